```python
import jax, jax.numpy as jnp
from jax import lax
import numpy as np

D_MODEL = 1024
BATCH = 8
SEQ = 4096
DEPTH = 4

N_MIXERS = 2
N_ATTN_LAYERS = (DEPTH + 1) // 2
N_HGRN_LAYERS = DEPTH // 2
HEAD_DIM = 64
N_Q_HEADS = D_MODEL // HEAD_DIM
N_KV_HEADS = N_Q_HEADS // 4
Q_PER_KV = N_Q_HEADS // N_KV_HEADS
WINDOW = 128
ATTN_BLOCK = WINDOW
ATTN_IN = (N_Q_HEADS + 2 * N_KV_HEADS) * HEAD_DIM
HG_EXPAND = 128
HG_HEADS = D_MODEL // HG_EXPAND
HG_K = HG_EXPAND
HG_V = D_MODEL // HG_HEADS
HG_FDIM = HG_HEADS * HG_K
HG_IDIM = HG_HEADS * HG_V
HG_IN = 2 * HG_FDIM + 2 * HG_IDIM
HG_CHUNK = 64
D_FF = 2816
CONV_W = 3
EPS = 1e-6

kernel_name = 'hybrid_swa_sink_alibi_hgrn2_convffn'


def rmsnorm(x, g):
    xf = x.astype(jnp.float32)
    y = xf * lax.rsqrt(jnp.mean(xf * xf, axis=-1, keepdims=True) + EPS)
    return (y * g.astype(jnp.float32)).astype(x.dtype)


def alibi_slopes():
    h = jnp.arange(1, N_Q_HEADS + 1, dtype=jnp.float32)
    return jnp.exp2(-8.0 * h / N_Q_HEADS)


def sliding_window_attention(h, w_in, w_out, sinks):
    B, S, _ = h.shape
    nblk = S // ATTN_BLOCK
    proj = h @ w_in
    q, k, v = jnp.split(proj, [N_Q_HEADS * HEAD_DIM, (N_Q_HEADS + N_KV_HEADS) * HEAD_DIM], axis=-1)
    q = q.reshape(B, nblk, ATTN_BLOCK, N_KV_HEADS, Q_PER_KV, HEAD_DIM)
    k = k.reshape(B, S, N_KV_HEADS, HEAD_DIM)
    v = v.reshape(B, S, N_KV_HEADS, HEAD_DIM)
    pad = jnp.zeros((B, ATTN_BLOCK, N_KV_HEADS, HEAD_DIM), k.dtype)
    kb = jnp.concatenate([pad, k], axis=1).reshape(B, nblk + 1, ATTN_BLOCK, N_KV_HEADS, HEAD_DIM)
    vb = jnp.concatenate([pad, v], axis=1).reshape(B, nblk + 1, ATTN_BLOCK, N_KV_HEADS, HEAD_DIM)
    kw = jnp.concatenate([kb[:, :-1], kb[:, 1:]], axis=2)
    vw = jnp.concatenate([vb[:, :-1], vb[:, 1:]], axis=2)
    scale = HEAD_DIM ** -0.5
    scores = jnp.einsum('bnqhgd,bnkhd->bnhgqk', q, kw).astype(jnp.float32) * scale
    qi = jnp.arange(ATTN_BLOCK)[:, None]
    ki = jnp.arange(2 * ATTN_BLOCK)[None, :]
    dist = qi + ATTN_BLOCK - ki
    in_window = (dist >= 0) & (dist < WINDOW)
    key_pos = (jnp.arange(nblk) * ATTN_BLOCK - ATTN_BLOCK)[:, None, None] + ki[None]
    valid = in_window[None] & (key_pos >= 0)
    slopes = alibi_slopes().reshape(N_KV_HEADS, Q_PER_KV)
    bias = -slopes[:, :, None, None] * dist.astype(jnp.float32)
    scores = jnp.where(valid[None, :, None, None], scores + bias, -jnp.inf)
    sink = jnp.broadcast_to(sinks.astype(jnp.float32).reshape(1, 1, N_KV_HEADS, Q_PER_KV, 1, 1),
                            scores.shape[:-1] + (1,))
    p = jax.nn.softmax(jnp.concatenate([scores, sink], axis=-1), axis=-1)[..., :-1]
    o = jnp.einsum('bnhgqk,bnkhd->bnqhgd', p.astype(vw.dtype), vw)
    o = o.reshape(B, S, N_Q_HEADS * HEAD_DIM)
    return o @ w_out


def hgrn2(h, w_in, w_out, norm_g, lb):
    B, S, _ = h.shape
    nc = S // HG_CHUNK
    proj = h @ w_in
    q, f, i, g = jnp.split(proj, [HG_FDIM, 2 * HG_FDIM, 2 * HG_FDIM + HG_IDIM], axis=-1)
    q = jax.nn.silu(q.astype(jnp.float32))
    f = f.astype(jnp.float32)
    lb = lb.astype(jnp.float32)
    log_f = jnp.logaddexp(jnp.log(lb), jnp.log1p(-lb) + jax.nn.log_sigmoid(f))
    k = (1.0 - lb) * jax.nn.sigmoid(-f)

    def to_chunks(t, d):
        return t.reshape(B, nc, HG_CHUNK, HG_HEADS, d).transpose(1, 0, 3, 2, 4)

    qc = to_chunks(q, HG_K)
    kc = to_chunks(k, HG_K)
    gc = to_chunks(log_f, HG_K)
    vc = to_chunks(i.astype(jnp.float32), HG_V)
    causal = jnp.tril(jnp.ones((HG_CHUNK, HG_CHUNK), bool))

    def step(state, inp):
        qt, kt, gt, vt = inp
        b = jnp.cumsum(gt, axis=2)
        rel = b[:, :, :, None, :] - b[:, :, None, :, :]
        decay = jnp.exp(jnp.where(causal[:, :, None], rel, -jnp.inf))
        a = jnp.einsum('bhtk,bhsk,bhtsk->bhts', qt, kt, decay)
        o = jnp.einsum('bhts,bhsv->bhtv', a, vt) + jnp.einsum('bhtk,bhkv->bhtv', qt * jnp.exp(b), state)
        b_last = b[:, :, -1:, :]
        new_state = jnp.exp(b_last[:, :, 0, :])[..., None] * state + \
            jnp.einsum('bhsk,bhsv->bhkv', kt * jnp.exp(b_last - b), vt)
        return new_state, o

    state0 = jnp.zeros((B, HG_HEADS, HG_K, HG_V), jnp.float32)
    _, o = lax.scan(step, state0, (qc, kc, gc, vc))
    o = o.transpose(1, 0, 3, 2, 4).reshape(B, S, HG_HEADS, HG_V)
    gate = g.reshape(B, S, HG_HEADS, HG_V).astype(jnp.float32)
    o = rmsnorm(o, norm_g) * jax.nn.silu(gate)
    return o.reshape(B, S, HG_IDIM).astype(h.dtype) @ w_out


def conv_ffn(h, w_up, conv_w, conv_b, w_down):
    S = h.shape[1]
    u = h @ w_up
    up = jnp.pad(u, ((0, 0), (CONV_W - 1, 0), (0, 0)))
    c = conv_b + conv_w[0] * up[:, 0:S]
    for j in range(1, CONV_W):
        c = c + conv_w[j] * up[:, j:j + S]
    gate, val = jnp.split(c, 2, axis=-1)
    return (jax.nn.silu(gate) * val) @ w_down


def setup_inputs(seed: int = 0) -> dict:
    key = jax.random.key(seed)
    ks = jax.random.split(key, 16)
    f32 = jnp.float32
    nrm = lambda k, shape, s: jax.random.normal(k, shape, f32) * s
    return {
        'x': nrm(ks[0], (BATCH, SEQ, D_MODEL), 1.0),
        'norm_mix': 1.0 + nrm(ks[1], (DEPTH, D_MODEL), 0.02),
        'norm_ffn': 1.0 + nrm(ks[2], (DEPTH, D_MODEL), 0.02),
        'norm_final': 1.0 + nrm(ks[3], (D_MODEL,), 0.02),
        'attn_w_in': nrm(ks[4], (N_ATTN_LAYERS, D_MODEL, ATTN_IN), D_MODEL ** -0.5),
        'attn_w_out': nrm(ks[5], (N_ATTN_LAYERS, N_Q_HEADS * HEAD_DIM, D_MODEL), (N_Q_HEADS * HEAD_DIM) ** -0.5),
        'attn_sinks': nrm(ks[6], (N_ATTN_LAYERS, N_Q_HEADS), 1.0),
        'hgrn_w_in': nrm(ks[7], (N_HGRN_LAYERS, D_MODEL, HG_IN), D_MODEL ** -0.5),
        'hgrn_w_out': nrm(ks[8], (N_HGRN_LAYERS, HG_IDIM, D_MODEL), HG_IDIM ** -0.5),
        'hgrn_norm': 1.0 + nrm(ks[9], (N_HGRN_LAYERS, HG_V), 0.02),
        'hgrn_lb_logits': 1.0 + nrm(ks[10], (DEPTH, HG_FDIM), 0.1),
        'ffn_w_up': nrm(ks[11], (DEPTH, D_MODEL, 2 * D_FF), D_MODEL ** -0.5),
        'ffn_conv_w': nrm(ks[12], (DEPTH, CONV_W, 2 * D_FF), CONV_W ** -0.5),
        'ffn_conv_b': nrm(ks[13], (DEPTH, 2 * D_FF), 0.01),
        'ffn_w_down': nrm(ks[14], (DEPTH, D_FF, D_MODEL), D_FF ** -0.5),
    }


def reference(x, norm_mix, norm_ffn, norm_final, attn_w_in, attn_w_out, attn_sinks,
              hgrn_w_in, hgrn_w_out, hgrn_norm, hgrn_lb_logits,
              ffn_w_up, ffn_conv_w, ffn_conv_b, ffn_w_down):
    s = jax.nn.softmax(hgrn_lb_logits.astype(jnp.float32), axis=0)
    lower_bounds = jnp.cumsum(s, axis=0) - s[0]
    h = x
    for layer in range(DEPTH):
        idx = layer // N_MIXERS
        hn = rmsnorm(h, norm_mix[layer])
        if layer % N_MIXERS == 0:
            h = h + sliding_window_attention(hn, attn_w_in[idx], attn_w_out[idx], attn_sinks[idx])
        else:
            h = h + hgrn2(hn, hgrn_w_in[idx], hgrn_w_out[idx], hgrn_norm[idx], lower_bounds[layer])
        h = h + conv_ffn(rmsnorm(h, norm_ffn[layer]), ffn_w_up[layer], ffn_conv_w[layer],
                         ffn_conv_b[layer], ffn_w_down[layer])
    return rmsnorm(h, norm_final)
```

```python
import functools

import jax
import jax.numpy as jnp
from jax import lax
from jax.experimental import pallas as pl
from jax.experimental.pallas import tpu as pltpu

F32 = jnp.float32
BF16 = jnp.bfloat16

D_MODEL = 1024
DEPTH = 4
EPS = 1e-6
HEAD_DIM = 64
N_Q_HEADS = 16
N_KV_HEADS = 4
Q_PER_KV = N_Q_HEADS // N_KV_HEADS
ATTN_BLOCK = 128
Q_COLS = N_Q_HEADS * HEAD_DIM
KV_COLS = 2 * N_KV_HEADS * HEAD_DIM
HG_HEADS = 8
HG_DIM = 128
HG_CHUNK = 64
D_FF = 2816
FF_CHUNK = 256
N_FF_CHUNKS = D_FF // FF_CHUNK
SUBLANES = 8
LANES = 128
VMEM_LIMIT_BYTES = 56 * 1024 * 1024

TOKEN_TILE = 512

_NT = (((1,), (1,)), ((), ()))
_TN = (((0,), (0,)), ((), ()))


def _rmsnorm(x, gain):
    ms = jnp.mean(x * x, axis=-1, keepdims=True)
    return x * lax.rsqrt(ms + EPS) * gain


def _silu(x):
    return x * jax.nn.sigmoid(x)


def _resident(shape):
    nd = len(shape)
    return pl.BlockSpec(shape, lambda i: (0,) * nd, pipeline_mode=pl.Buffered(1))


def _params():
    return pltpu.CompilerParams(dimension_semantics=("arbitrary",), vmem_limit_bytes=VMEM_LIMIT_BYTES)


def _qkv_kernel(h_ref, gain_ref, w_ref, o_ref):
    xn = _rmsnorm(h_ref[...], gain_ref[...]).astype(BF16)
    o_ref[...] = jnp.dot(xn, w_ref[...], preferred_element_type=F32).astype(o_ref.dtype)


def _attn_kernel(q_ref, kv_ref, kvp_ref, h_ref, wo_ref, sink_ref, o_ref, kv_scr, att_scr, *, tiles_per_seq):
    i = pl.program_id(0)
    blk = ATTN_BLOCK
    kv_scr[0:blk, :] = kvp_ref[...]
    kv_scr[blk:, :] = kv_ref[...]
    seq_first = (i % tiles_per_seq) == 0

    qi = lax.broadcasted_iota(jnp.int32, (blk, 2 * blk), 0)
    ki = lax.broadcasted_iota(jnp.int32, (blk, 2 * blk), 1)
    dist = qi + blk - ki
    in_window = (dist >= 0) & (dist < blk)
    distf = dist.astype(F32)
    koff = N_KV_HEADS * HEAD_DIM

    def block_body(n, carry):
        r0 = pl.multiple_of(n * blk, blk)
        q = q_ref[pl.ds(r0, blk), :]
        kvw = kv_scr[pl.ds(r0, 2 * blk), :]
        no_prev = seq_first & (n == 0)
        valid = in_window & ((ki >= blk) | jnp.logical_not(no_prev))
        outs = []
        for qh in range(N_Q_HEADS):
            hk = qh // Q_PER_KV
            slope = 2.0 ** (-8.0 * (qh + 1) / N_Q_HEADS)
            qs = q[:, qh * HEAD_DIM:(qh + 1) * HEAD_DIM] * (HEAD_DIM ** -0.5)
            kh = kvw[:, hk * HEAD_DIM:(hk + 1) * HEAD_DIM]
            vh = kvw[:, koff + hk * HEAD_DIM:koff + (hk + 1) * HEAD_DIM]
            s = lax.dot_general(qs, kh, _NT, preferred_element_type=F32)
            s = jnp.where(valid, s - slope * distf, -jnp.inf)
            sink = sink_ref[0, qh]
            m = jnp.maximum(jnp.max(s, axis=-1, keepdims=True), sink)
            p = jnp.exp(s - m)
            den = jnp.sum(p, axis=-1, keepdims=True) + jnp.exp(sink - m)
            o = jnp.dot(p.astype(BF16), vh, preferred_element_type=F32) / den
            outs.append(o)
        att_scr[pl.ds(r0, blk), :] = jnp.concatenate(outs, axis=-1).astype(BF16)
        return carry

    lax.fori_loop(0, TOKEN_TILE // blk, block_body, 0)
    o_ref[...] = h_ref[...] + jnp.dot(att_scr[...], wo_ref[...], preferred_element_type=F32)


def _attention_layer(h, gain, w_in, w_out, sinks, seq_len):
    n = h.shape[0]
    tm = TOKEN_TILE
    n_tiles = n // tm
    qkv = pl.pallas_call(
        _qkv_kernel,
        out_shape=jax.ShapeDtypeStruct((n, Q_COLS + KV_COLS), BF16),
        grid=(n_tiles,),
        in_specs=[pl.BlockSpec((tm, D_MODEL), lambda i: (i, 0)),
                  _resident((1, D_MODEL)),
                  _resident((D_MODEL, Q_COLS + KV_COLS))],
        out_specs=pl.BlockSpec((tm, Q_COLS + KV_COLS), lambda i: (i, 0)),
        compiler_params=_params(),
        name="attn_qkv",
    )(h, gain.reshape(1, D_MODEL), w_in.astype(BF16))

    blocks_per_tile = tm // ATTN_BLOCK
    kv_col_block = Q_COLS // KV_COLS
    return pl.pallas_call(
        functools.partial(_attn_kernel, tiles_per_seq=seq_len // tm),
        out_shape=jax.ShapeDtypeStruct((n, D_MODEL), F32),
        grid=(n_tiles,),
        in_specs=[pl.BlockSpec((tm, Q_COLS), lambda i: (i, 0)),
                  pl.BlockSpec((tm, KV_COLS), lambda i: (i, kv_col_block)),
                  pl.BlockSpec((ATTN_BLOCK, KV_COLS),
                               lambda i: (jnp.maximum(i * blocks_per_tile - 1, 0), kv_col_block)),
                  pl.BlockSpec((tm, D_MODEL), lambda i: (i, 0)),
                  _resident((Q_COLS, D_MODEL)),
                  pl.BlockSpec(memory_space=pltpu.SMEM)],
        out_specs=pl.BlockSpec((tm, D_MODEL), lambda i: (i, 0)),
        scratch_shapes=[pltpu.VMEM((tm + ATTN_BLOCK, KV_COLS), BF16),
                        pltpu.VMEM((tm, Q_COLS), BF16)],
        compiler_params=_params(),
        name="attn_core",
    )(qkv, qkv, qkv, h, w_out.astype(BF16), sinks.reshape(1, N_Q_HEADS).astype(F32))


def _row_of_group(b, group, j):
    c = b.shape[0]
    b3 = b.reshape(c // group, group, LANES)
    return jnp.broadcast_to(b3[:, j:j + 1, :], b3.shape).reshape(c, LANES)


def _hgrn_chunk(q, k, g, v, st, row, level_of):
    c = HG_CHUNK
    b = g
    for sh in (1, 2, 4, 8, 16, 32):
        b = b + jnp.where(row >= sh, pltpu.roll(b, sh, 0), 0.0)
    sub = row % SUBLANES
    b_mid = (
        jnp.where(sub % 2 == 1, pltpu.roll(b, 1, 0), b),
        jnp.where(sub < 4, _row_of_group(b, 8, 1), _row_of_group(b, 8, 5)),
        _row_of_group(b, 8, 3),
        _row_of_group(b, 16, 7),
        _row_of_group(b, 32, 15),
        _row_of_group(b, 64, 31),
    )
    a = lax.dot_general(q.astype(BF16), k.astype(BF16), _NT, preferred_element_type=F32)
    a = jnp.where(level_of == 0, a, 0.0)
    for lvl, bm in enumerate(b_mid):
        e = jnp.exp(-jnp.abs(b - bm))
        a_l = lax.dot_general((q * e).astype(BF16), (k * e).astype(BF16), _NT, preferred_element_type=F32)
        a = jnp.where(level_of == lvl + 1, a_l, a)
    o = jnp.dot(a.astype(BF16), v, preferred_element_type=F32)
    o = o + lax.dot_general((q * jnp.exp(b)).astype(BF16), st.astype(BF16), _NT, preferred_element_type=F32)
    b_last = b[c - 1:c, :]
    k_dec = (k * jnp.exp(b_last - b)).astype(BF16)
    st_new = st * jnp.exp(b_last) + lax.dot_general(v, k_dec, _TN, preferred_element_type=F32)
    return o, st_new


def _hgrn_kernel(h_ref, gain_ref, wq_ref, wf_ref, wi_ref, wg_ref, lbl_ref, ng_ref, wo_ref, o_ref,
                 q_scr, k_scr, g_scr, v_scr, gate_scr, y_scr, yb_scr, st_scr, *, layer, tiles_per_seq):
    i = pl.program_id(0)
    x = h_ref[...]
    xn = _rmsnorm(x, gain_ref[...]).astype(BF16)

    logits = lbl_ref[...]
    ex = jnp.exp(logits - jnp.max(logits, axis=0, keepdims=True))
    sm = ex / jnp.sum(ex, axis=0, keepdims=True)
    cum = sm[0:1, :]
    for r in range(1, layer + 1):
        cum = cum + sm[r:r + 1, :]
    lb = cum - sm[0:1, :]
    log_lb = jnp.log(lb)
    log_1m = jnp.log1p(-lb)

    qp = jnp.dot(xn, wq_ref[...], preferred_element_type=F32)
    q_scr[...] = _silu(qp)
    f = jnp.dot(xn, wf_ref[...], preferred_element_type=F32)
    log_sig = jnp.minimum(f, 0.0) - jnp.log1p(jnp.exp(-jnp.abs(f)))
    u = log_1m + log_sig
    g_scr[...] = jnp.maximum(log_lb, u) + jnp.log1p(jnp.exp(-jnp.abs(log_lb - u)))
    k_scr[...] = (1.0 - lb) * jax.nn.sigmoid(-f)
    v_scr[...] = jnp.dot(xn, wi_ref[...], preferred_element_type=F32).astype(BF16)
    gate_scr[...] = jnp.dot(xn, wg_ref[...], preferred_element_type=F32)

    @pl.when((i % tiles_per_seq) == 0)
    def _():
        st_scr[...] = jnp.zeros_like(st_scr)

    c = HG_CHUNK
    row = lax.broadcasted_iota(jnp.int32, (c, LANES), 0)
    ti = lax.broadcasted_iota(jnp.int32, (c, c), 0)
    si = lax.broadcasted_iota(jnp.int32, (c, c), 1)
    diff = ti ^ si
    level_of = sum((diff >= (1 << bit)).astype(jnp.int32) for bit in range(6))
    level_of = jnp.where(si > ti, -1, level_of)

    def chunk_body(ci, carry):
        r0 = pl.multiple_of(ci * c, c)
        for hh in range(HG_HEADS):
            cs = slice(hh * HG_DIM, (hh + 1) * HG_DIM)
            o, st_new = _hgrn_chunk(q_scr[pl.ds(r0, c), cs], k_scr[pl.ds(r0, c), cs], g_scr[pl.ds(r0, c), cs],
                                    v_scr[pl.ds(r0, c), cs], st_scr[hh], row, level_of)
            st_scr[hh] = st_new
            y_scr[pl.ds(r0, c), cs] = o
        return carry

    lax.fori_loop(0, TOKEN_TILE // c, chunk_body, 0)

    for hh in range(HG_HEADS):
        cs = slice(hh * HG_DIM, (hh + 1) * HG_DIM)
        yb_scr[:, cs] = (_rmsnorm(y_scr[:, cs], ng_ref[...]) * _silu(gate_scr[:, cs])).astype(BF16)
    o_ref[...] = x + jnp.dot(yb_scr[...], wo_ref[...], preferred_element_type=F32)


def _hgrn_layer(h, gain, w_in, w_out, norm_g, lb_logits, layer, seq_len):
    n = h.shape[0]
    tm = TOKEN_TILE
    d = D_MODEL
    w = w_in.astype(BF16)
    tile = pl.BlockSpec((tm, d), lambda i: (i, 0))
    return pl.pallas_call(
        functools.partial(_hgrn_kernel, layer=layer, tiles_per_seq=seq_len // tm),
        out_shape=jax.ShapeDtypeStruct((n, d), F32),
        grid=(n // tm,),
        in_specs=[tile, _resident((1, d)),
                  _resident((d, d)), _resident((d, d)), _resident((d, d)), _resident((d, d)),
                  _resident((DEPTH, d)), _resident((1, HG_DIM)), _resident((d, d))],
        out_specs=tile,
        scratch_shapes=[pltpu.VMEM((tm, d), F32), pltpu.VMEM((tm, d), F32), pltpu.VMEM((tm, d), F32),
                        pltpu.VMEM((tm, d), BF16), pltpu.VMEM((tm, d), F32), pltpu.VMEM((tm, d), F32),
                        pltpu.VMEM((tm, d), BF16), pltpu.VMEM((HG_HEADS, HG_DIM, HG_DIM), F32)],
        compiler_params=_params(),
        name=f"hgrn_l{layer}",
    )(h, gain.reshape(1, d), w[:, 0:d], w[:, d:2 * d], w[:, 2 * d:3 * d], w[:, 3 * d:4 * d],
      lb_logits.astype(F32), norm_g.reshape(1, HG_DIM), w_out.astype(BF16))


def _ffn_kernel(h_ref, gain_ref, wg_ref, wv_ref, cwg_ref, cwv_ref, cbg_ref, cbv_ref, wd_ref, gfin_ref, o_ref,
                xn_scr, acc_scr, halo_g, halo_v, *, tiles_per_seq, final_norm):
    i = pl.program_id(0)
    tm = TOKEN_TILE
    x = h_ref[...]
    xn_scr[...] = _rmsnorm(x, gain_ref[...]).astype(BF16)
    acc_scr[...] = jnp.zeros_like(acc_scr)
    seq_first = (i % tiles_per_seq) == 0
    row8 = lax.broadcasted_iota(jnp.int32, (SUBLANES, FF_CHUNK), 0)

    def conv(u, w, bias, halo_ref, j):
        prev = jnp.where(seq_first, 0.0, halo_ref[j])
        halo_ref[j] = u[tm - SUBLANES:, :]
        p1 = prev[SUBLANES - 1:SUBLANES, :]
        p2 = prev[SUBLANES - 2:SUBLANES - 1, :]
        u1 = pltpu.roll(u, 1, 0)
        u2 = pltpu.roll(u, 2, 0)
        top1 = jnp.where(row8 == 0, p1, u1[:SUBLANES])
        top2 = jnp.where(row8 == 0, p2, jnp.where(row8 == 1, p1, u2[:SUBLANES]))
        u1 = jnp.concatenate([top1, u1[SUBLANES:]], axis=0)
        u2 = jnp.concatenate([top2, u2[SUBLANES:]], axis=0)
        return bias + w[0:1, :] * u2 + w[1:2, :] * u1 + w[2:3, :] * u

    def chunk_body(j, carry):
        xn = xn_scr[...]
        ug = jnp.dot(xn, wg_ref[j], preferred_element_type=F32)
        uv = jnp.dot(xn, wv_ref[j], preferred_element_type=F32)
        cg = conv(ug, cwg_ref[j], cbg_ref[j], halo_g, j)
        cv = conv(uv, cwv_ref[j], cbv_ref[j], halo_v, j)
        act = (_silu(cg) * cv).astype(BF16)
        acc_scr[...] += jnp.dot(act, wd_ref[j], preferred_element_type=F32)
        return carry

    lax.fori_loop(0, N_FF_CHUNKS, chunk_body, 0)
    y = x + acc_scr[...]
    if final_norm:
        y = _rmsnorm(y, gfin_ref[...])
    o_ref[...] = y


def _chunk_cols(w):
    r = w.shape[0]
    return w.reshape(r, N_FF_CHUNKS, FF_CHUNK).transpose(1, 0, 2)


def _ffn_layer(h, gain, w_up, conv_w, conv_b, w_down, gain_final, final_norm, seq_len):
    n = h.shape[0]
    tm = TOKEN_TILE
    d = D_MODEL
    wu = w_up.astype(BF16)
    cb = conv_b.reshape(1, 2 * D_FF)
    tile = pl.BlockSpec((tm, d), lambda i: (i, 0))
    nc, fc = N_FF_CHUNKS, FF_CHUNK
    return pl.pallas_call(
        functools.partial(_ffn_kernel, tiles_per_seq=seq_len // tm, final_norm=final_norm),
        out_shape=jax.ShapeDtypeStruct((n, d), F32),
        grid=(n // tm,),
        in_specs=[tile, _resident((1, d)),
                  _resident((nc, d, fc)), _resident((nc, d, fc)),
                  _resident((nc, 3, fc)), _resident((nc, 3, fc)),
                  _resident((nc, 1, fc)), _resident((nc, 1, fc)),
                  _resident((nc, fc, d)), _resident((1, d))],
        out_specs=tile,
        scratch_shapes=[pltpu.VMEM((tm, d), BF16), pltpu.VMEM((tm, d), F32),
                        pltpu.VMEM((nc, SUBLANES, fc), F32), pltpu.VMEM((nc, SUBLANES, fc), F32)],
        compiler_params=_params(),
        name="conv_ffn",
    )(h, gain.reshape(1, d),
      _chunk_cols(wu[:, :D_FF]), _chunk_cols(wu[:, D_FF:]),
      _chunk_cols(conv_w[:, :D_FF]), _chunk_cols(conv_w[:, D_FF:]),
      _chunk_cols(cb[:, :D_FF]), _chunk_cols(cb[:, D_FF:]),
      w_down.astype(BF16).reshape(nc, fc, d), gain_final.reshape(1, d))


def kernel(x, norm_mix, norm_ffn, norm_final, attn_w_in, attn_w_out, attn_sinks, hgrn_w_in, hgrn_w_out,
           hgrn_norm, hgrn_lb_logits, ffn_w_up, ffn_conv_w, ffn_conv_b, ffn_w_down):
    bsz, seq_len, d = x.shape
    assert d == D_MODEL and seq_len % TOKEN_TILE == 0
    h = x.reshape(bsz * seq_len, d)
    for layer in range(DEPTH):
        idx = layer // 2
        if layer % 2 == 0:
            h = _attention_layer(h, norm_mix[layer], attn_w_in[idx], attn_w_out[idx], attn_sinks[idx], seq_len)
        else:
            h = _hgrn_layer(h, norm_mix[layer], hgrn_w_in[idx], hgrn_w_out[idx], hgrn_norm[idx],
                            hgrn_lb_logits, layer, seq_len)
        h = _ffn_layer(h, norm_ffn[layer], ffn_w_up[layer], ffn_conv_w[layer], ffn_conv_b[layer],
                       ffn_w_down[layer], norm_final, layer == DEPTH - 1, seq_len)
    return h.reshape(bsz, seq_len, d)
```

```python
import functools

import jax
import jax.numpy as jnp
from jax import lax
from jax.experimental import pallas as pl
from jax.experimental.pallas import tpu as pltpu

F32 = jnp.float32
BF16 = jnp.bfloat16

D_MODEL = 1024
DEPTH = 4
EPS = 1e-6
HEAD_DIM = 64
N_Q_HEADS = 16
N_KV_HEADS = 4
Q_PER_KV = N_Q_HEADS // N_KV_HEADS
ATTN_BLOCK = 128
Q_COLS = N_Q_HEADS * HEAD_DIM
KV_COLS = 2 * N_KV_HEADS * HEAD_DIM
HG_HEADS = 8
HG_DIM = 128
HG_CHUNK = 64
D_FF = 2816
FF_CHUNK = 256
N_FF_CHUNKS = D_FF // FF_CHUNK
SUBLANES = 8
LANES = 128
VMEM_LIMIT_BYTES = 56 * 1024 * 1024

TOKEN_TILE = 512

_NT = (((1,), (1,)), ((), ()))
_TN = (((0,), (0,)), ((), ()))


def _rmsnorm(x, gain):
    ms = jnp.mean(x * x, axis=-1, keepdims=True)
    return x * lax.rsqrt(ms + EPS) * gain


def _silu(x):
    return x * jax.nn.sigmoid(x)


def _resident(shape):
    nd = len(shape)
    return pl.BlockSpec(shape, lambda i: (0,) * nd, pipeline_mode=pl.Buffered(1))


def _params():
    return pltpu.CompilerParams(dimension_semantics=("arbitrary",), vmem_limit_bytes=VMEM_LIMIT_BYTES)


def _qkv_kernel(h_ref, gain_ref, w_ref, o_ref):
    xn = _rmsnorm(h_ref[...], gain_ref[...]).astype(BF16)
    o_ref[...] = jnp.dot(xn, w_ref[...], preferred_element_type=F32).astype(o_ref.dtype)


def _attn_kernel(q_ref, kv_ref, kvp_ref, h_ref, wo_ref, sink_ref, o_ref, kv_scr, att_scr, *, tiles_per_seq):
    i = pl.program_id(0)
    blk = ATTN_BLOCK
    kv_scr[0:blk, :] = kvp_ref[...]
    kv_scr[blk:, :] = kv_ref[...]
    seq_first = (i % tiles_per_seq) == 0

    qi = lax.broadcasted_iota(jnp.int32, (blk, 2 * blk), 0)
    ki = lax.broadcasted_iota(jnp.int32, (blk, 2 * blk), 1)
    dist = qi + blk - ki
    in_window = (dist >= 0) & (dist < blk)
    distf = dist.astype(F32)
    koff = N_KV_HEADS * HEAD_DIM

    def block_body(n, carry):
        r0 = pl.multiple_of(n * blk, blk)
        q = q_ref[pl.ds(r0, blk), :]
        kvw = kv_scr[pl.ds(r0, 2 * blk), :]
        no_prev = seq_first & (n == 0)
        valid = in_window & ((ki >= blk) | jnp.logical_not(no_prev))
        outs = []
        for qh in range(N_Q_HEADS):
            hk = qh // Q_PER_KV
            slope = 2.0 ** (-8.0 * (qh + 1) / N_Q_HEADS)
            qs = q[:, qh * HEAD_DIM:(qh + 1) * HEAD_DIM] * (HEAD_DIM ** -0.5)
            kh = kvw[:, hk * HEAD_DIM:(hk + 1) * HEAD_DIM]
            vh = kvw[:, koff + hk * HEAD_DIM:koff + (hk + 1) * HEAD_DIM]
            s = lax.dot_general(qs, kh, _NT, preferred_element_type=F32)
            s = jnp.where(valid, s - slope * distf, -jnp.inf)
            sink = sink_ref[0, qh]
            m = jnp.maximum(jnp.max(s, axis=-1, keepdims=True), sink)
            p = jnp.exp(s - m)
            den = jnp.sum(p, axis=-1, keepdims=True) + jnp.exp(sink - m)
            o = jnp.dot(p.astype(BF16), vh, preferred_element_type=F32) / den
            outs.append(o)
        att_scr[pl.ds(r0, blk), :] = jnp.concatenate(outs, axis=-1).astype(BF16)
        return carry

    lax.fori_loop(0, TOKEN_TILE // blk, block_body, 0)
    o_ref[...] = h_ref[...] + jnp.dot(att_scr[...], wo_ref[...], preferred_element_type=F32)


def _attention_layer(h, gain, w_in, w_out, sinks, seq_len):
    n = h.shape[0]
    tm = TOKEN_TILE
    n_tiles = n // tm
    qkv = pl.pallas_call(
        _qkv_kernel,
        out_shape=jax.ShapeDtypeStruct((n, Q_COLS + KV_COLS), BF16),
        grid=(n_tiles,),
        in_specs=[pl.BlockSpec((tm, D_MODEL), lambda i: (i, 0)),
                  _resident((1, D_MODEL)),
                  _resident((D_MODEL, Q_COLS + KV_COLS))],
        out_specs=pl.BlockSpec((tm, Q_COLS + KV_COLS), lambda i: (i, 0)),
        compiler_params=_params(),
        name="attn_qkv",
    )(h, gain.reshape(1, D_MODEL), w_in.astype(BF16))

    blocks_per_tile = tm // ATTN_BLOCK
    kv_col_block = Q_COLS // KV_COLS
    return pl.pallas_call(
        functools.partial(_attn_kernel, tiles_per_seq=seq_len // tm),
        out_shape=jax.ShapeDtypeStruct((n, D_MODEL), F32),
        grid=(n_tiles,),
        in_specs=[pl.BlockSpec((tm, Q_COLS), lambda i: (i, 0)),
                  pl.BlockSpec((tm, KV_COLS), lambda i: (i, kv_col_block)),
                  pl.BlockSpec((ATTN_BLOCK, KV_COLS),
                               lambda i: (jnp.maximum(i * blocks_per_tile - 1, 0), kv_col_block)),
                  pl.BlockSpec((tm, D_MODEL), lambda i: (i, 0)),
                  _resident((Q_COLS, D_MODEL)),
                  pl.BlockSpec(memory_space=pltpu.SMEM)],
        out_specs=pl.BlockSpec((tm, D_MODEL), lambda i: (i, 0)),
        scratch_shapes=[pltpu.VMEM((tm + ATTN_BLOCK, KV_COLS), BF16),
                        pltpu.VMEM((tm, Q_COLS), BF16)],
        compiler_params=_params(),
        name="attn_core",
    )(qkv, qkv, qkv, h, w_out.astype(BF16), sinks.reshape(1, N_Q_HEADS).astype(F32))


def _row_of_group(b, group, j):
    c = b.shape[0]
    b3 = b.reshape(c // group, group, LANES)
    return jnp.broadcast_to(b3[:, j:j + 1, :], b3.shape).reshape(c, LANES)


def _hgrn_chunk(q, k, g, v, st, row, level_of):
    c = HG_CHUNK
    b = g
    for sh in (1, 2, 4, 8, 16, 32):
        b = b + jnp.where(row >= sh, pltpu.roll(b, sh, 0), 0.0)
    sub = row % SUBLANES
    b_mid = (
        jnp.where(sub % 2 == 1, pltpu.roll(b, 1, 0), b),
        jnp.where(sub < 4, _row_of_group(b, 8, 1), _row_of_group(b, 8, 5)),
        _row_of_group(b, 8, 3),
        _row_of_group(b, 16, 7),
        _row_of_group(b, 32, 15),
        _row_of_group(b, 64, 31),
    )
    a = lax.dot_general(q.astype(BF16), k.astype(BF16), _NT, preferred_element_type=F32)
    a = jnp.where(level_of == 0, a, 0.0)
    for lvl, bm in enumerate(b_mid):
        e = jnp.exp(-jnp.abs(b - bm))
        a_l = lax.dot_general((q * e).astype(BF16), (k * e).astype(BF16), _NT, preferred_element_type=F32)
        a = jnp.where(level_of == lvl + 1, a_l, a)
    o = jnp.dot(a.astype(BF16), v, preferred_element_type=F32)
    o = o + lax.dot_general((q * jnp.exp(b)).astype(BF16), st.astype(BF16), _NT, preferred_element_type=F32)
    b_last = b[c - 1:c, :]
    k_dec = (k * jnp.exp(b_last - b)).astype(BF16)
    st_new = st * jnp.exp(b_last) + lax.dot_general(v, k_dec, _TN, preferred_element_type=F32)
    return o, st_new


def _hgrn_kernel(h_ref, gain_ref, wq_ref, wf_ref, wi_ref, wg_ref, lbl_ref, ng_ref, wo_ref, o_ref,
                 q_scr, k_scr, g_scr, v_scr, gate_scr, y_scr, yb_scr, st_scr, *, layer, tiles_per_seq):
    i = pl.program_id(0)
    x = h_ref[...]
    xn = _rmsnorm(x, gain_ref[...]).astype(BF16)

    logits = lbl_ref[...]
    ex = jnp.exp(logits - jnp.max(logits, axis=0, keepdims=True))
    sm = ex / jnp.sum(ex, axis=0, keepdims=True)
    cum = sm[0:1, :]
    for r in range(1, layer + 1):
        cum = cum + sm[r:r + 1, :]
    lb = cum - sm[0:1, :]
    log_lb = jnp.log(lb)
    log_1m = jnp.log1p(-lb)

    qp = jnp.dot(xn, wq_ref[...], preferred_element_type=F32)
    q_scr[...] = _silu(qp)
    f = jnp.dot(xn, wf_ref[...], preferred_element_type=F32)
    log_sig = jnp.minimum(f, 0.0) - jnp.log1p(jnp.exp(-jnp.abs(f)))
    u = log_1m + log_sig
    g_scr[...] = jnp.maximum(log_lb, u) + jnp.log1p(jnp.exp(-jnp.abs(log_lb - u)))
    k_scr[...] = (1.0 - lb) * jax.nn.sigmoid(-f)
    v_scr[...] = jnp.dot(xn, wi_ref[...], preferred_element_type=F32).astype(BF16)
    gate_scr[...] = jnp.dot(xn, wg_ref[...], preferred_element_type=F32)

    @pl.when((i % tiles_per_seq) == 0)
    def _():
        st_scr[...] = jnp.zeros_like(st_scr)

    c = HG_CHUNK
    row = lax.broadcasted_iota(jnp.int32, (c, LANES), 0)
    ti = lax.broadcasted_iota(jnp.int32, (c, c), 0)
    si = lax.broadcasted_iota(jnp.int32, (c, c), 1)
    diff = ti ^ si
    level_of = sum((diff >= (1 << bit)).astype(jnp.int32) for bit in range(6))
    level_of = jnp.where(si > ti, -1, level_of)

    def chunk_body(ci, carry):
        r0 = pl.multiple_of(ci * c, c)
        for hh in range(HG_HEADS):
            cs = slice(hh * HG_DIM, (hh + 1) * HG_DIM)
            o, st_new = _hgrn_chunk(q_scr[pl.ds(r0, c), cs], k_scr[pl.ds(r0, c), cs], g_scr[pl.ds(r0, c), cs],
                                    v_scr[pl.ds(r0, c), cs], st_scr[hh], row, level_of)
            st_scr[hh] = st_new
            y_scr[pl.ds(r0, c), cs] = o
        return carry

    lax.fori_loop(0, TOKEN_TILE // c, chunk_body, 0)

    for hh in range(HG_HEADS):
        cs = slice(hh * HG_DIM, (hh + 1) * HG_DIM)
        yb_scr[:, cs] = (_rmsnorm(y_scr[:, cs], ng_ref[...]) * _silu(gate_scr[:, cs])).astype(BF16)
    o_ref[...] = x + jnp.dot(yb_scr[...], wo_ref[...], preferred_element_type=F32)


def _hgrn_layer(h, gain, w_in, w_out, norm_g, lb_logits, layer, seq_len):
    n = h.shape[0]
    tm = TOKEN_TILE
    d = D_MODEL
    w = w_in.astype(BF16)
    tile = pl.BlockSpec((tm, d), lambda i: (i, 0))
    return pl.pallas_call(
        functools.partial(_hgrn_kernel, layer=layer, tiles_per_seq=seq_len // tm),
        out_shape=jax.ShapeDtypeStruct((n, d), F32),
        grid=(n // tm,),
        in_specs=[tile, _resident((1, d)),
                  _resident((d, d)), _resident((d, d)), _resident((d, d)), _resident((d, d)),
                  _resident((DEPTH, d)), _resident((1, HG_DIM)), _resident((d, d))],
        out_specs=tile,
        scratch_shapes=[pltpu.VMEM((tm, d), F32), pltpu.VMEM((tm, d), F32), pltpu.VMEM((tm, d), F32),
                        pltpu.VMEM((tm, d), BF16), pltpu.VMEM((tm, d), F32), pltpu.VMEM((tm, d), F32),
                        pltpu.VMEM((tm, d), BF16), pltpu.VMEM((HG_HEADS, HG_DIM, HG_DIM), F32)],
        compiler_params=_params(),
        name=f"hgrn_l{layer}",
    )(h, gain.reshape(1, d), w[:, 0:d], w[:, d:2 * d], w[:, 2 * d:3 * d], w[:, 3 * d:4 * d],
      lb_logits.astype(F32), norm_g.reshape(1, HG_DIM), w_out.astype(BF16))


def _ffn_kernel(h_ref, gain_ref, wg_ref, wv_ref, cwg_ref, cwv_ref, cbg_ref, cbv_ref, wd_ref, gfin_ref, o_ref,
                xn_scr, acc_scr, u_scr, halo_scr, *, tiles_per_seq, final_norm):
    i = pl.program_id(0)
    tm = TOKEN_TILE
    hr = SUBLANES
    n_lane_tiles = FF_CHUNK // LANES
    x = h_ref[...]
    xn_scr[...] = _rmsnorm(x, gain_ref[...]).astype(BF16)
    seq_first = (i % tiles_per_seq) == 0

    def up_proj(j):
        xn = xn_scr[...]
        for part, w_ref in ((0, wg_ref), (1, wv_ref)):
            u = jnp.dot(xn, w_ref[j], preferred_element_type=F32)
            for c in range(n_lane_tiles):
                u_scr[j % 2, part, c, hr:, :] = u[:, c * LANES:(c + 1) * LANES]

    def conv(j, part, w_ref, b_ref):
        w = w_ref[j]
        bias = b_ref[j]
        cols = []
        for c in range(n_lane_tiles):
            us = u_scr.at[j % 2, part, c]
            us[0:hr, :] = jnp.where(seq_first, 0.0, halo_scr[j, part, c])
            halo_scr[j, part, c] = us[tm:tm + hr, :]
            lanes = slice(c * LANES, (c + 1) * LANES)
            cols.append(bias[:, lanes] + w[0:1, lanes] * us[hr - 2:hr - 2 + tm, :]
                        + w[1:2, lanes] * us[hr - 1:hr - 1 + tm, :] + w[2:3, lanes] * us[hr:hr + tm, :])
        return jnp.concatenate(cols, axis=-1)

    up_proj(0)
    for j in range(N_FF_CHUNKS):
        if j + 1 < N_FF_CHUNKS:
            up_proj(j + 1)
        act = (_silu(conv(j, 0, cwg_ref, cbg_ref)) * conv(j, 1, cwv_ref, cbv_ref)).astype(BF16)
        dn = jnp.dot(act, wd_ref[j], preferred_element_type=F32)
        if j == 0:
            acc_scr[...] = dn
        else:
            acc_scr[...] += dn
    y = x + acc_scr[...]
    if final_norm:
        y = _rmsnorm(y, gfin_ref[...])
    o_ref[...] = y


def _chunk_cols(w):
    r = w.shape[0]
    return w.reshape(r, N_FF_CHUNKS, FF_CHUNK).transpose(1, 0, 2)


def _ffn_layer(h, gain, w_up, conv_w, conv_b, w_down, gain_final, final_norm, seq_len):
    n = h.shape[0]
    tm = TOKEN_TILE
    d = D_MODEL
    wu = w_up.astype(BF16)
    cb = conv_b.reshape(1, 2 * D_FF)
    tile = pl.BlockSpec((tm, d), lambda i: (i, 0))
    nc, fc = N_FF_CHUNKS, FF_CHUNK
    return pl.pallas_call(
        functools.partial(_ffn_kernel, tiles_per_seq=seq_len // tm, final_norm=final_norm),
        out_shape=jax.ShapeDtypeStruct((n, d), F32),
        grid=(n // tm,),
        in_specs=[tile, _resident((1, d)),
                  _resident((nc, d, fc)), _resident((nc, d, fc)),
                  _resident((nc, 3, fc)), _resident((nc, 3, fc)),
                  _resident((nc, 1, fc)), _resident((nc, 1, fc)),
                  _resident((nc, fc, d)), _resident((1, d))],
        out_specs=tile,
        scratch_shapes=[pltpu.VMEM((tm, d), BF16), pltpu.VMEM((tm, d), F32),
                        pltpu.VMEM((2, 2, fc // LANES, tm + SUBLANES, LANES), F32),
                        pltpu.VMEM((nc, 2, fc // LANES, SUBLANES, LANES), F32)],
        compiler_params=_params(),
        name="conv_ffn",
    )(h, gain.reshape(1, d),
      _chunk_cols(wu[:, :D_FF]), _chunk_cols(wu[:, D_FF:]),
      _chunk_cols(conv_w[:, :D_FF]), _chunk_cols(conv_w[:, D_FF:]),
      _chunk_cols(cb[:, :D_FF]), _chunk_cols(cb[:, D_FF:]),
      w_down.astype(BF16).reshape(nc, fc, d), gain_final.reshape(1, d))


def kernel(x, norm_mix, norm_ffn, norm_final, attn_w_in, attn_w_out, attn_sinks, hgrn_w_in, hgrn_w_out,
           hgrn_norm, hgrn_lb_logits, ffn_w_up, ffn_conv_w, ffn_conv_b, ffn_w_down):
    bsz, seq_len, d = x.shape
    assert d == D_MODEL and seq_len % TOKEN_TILE == 0
    h = x.reshape(bsz * seq_len, d)
    for layer in range(DEPTH):
        idx = layer // 2
        if layer % 2 == 0:
            h = _attention_layer(h, norm_mix[layer], attn_w_in[idx], attn_w_out[idx], attn_sinks[idx], seq_len)
        else:
            h = _hgrn_layer(h, norm_mix[layer], hgrn_w_in[idx], hgrn_w_out[idx], hgrn_norm[idx],
                            hgrn_lb_logits, layer, seq_len)
        h = _ffn_layer(h, norm_ffn[layer], ffn_w_up[layer], ffn_conv_w[layer], ffn_conv_b[layer],
                       ffn_w_down[layer], norm_final, layer == DEPTH - 1, seq_len)
    return h.reshape(bsz, seq_len, d)
```

```python
import functools

import jax
import jax.numpy as jnp
from jax import lax
from jax.experimental import pallas as pl
from jax.experimental.pallas import tpu as pltpu

F32 = jnp.float32
BF16 = jnp.bfloat16

D_MODEL = 1024
DEPTH = 4
EPS = 1e-6
HEAD_DIM = 64
N_Q_HEADS = 16
N_KV_HEADS = 4
Q_PER_KV = N_Q_HEADS // N_KV_HEADS
ATTN_BLOCK = 128
Q_COLS = N_Q_HEADS * HEAD_DIM
KV_COLS = 2 * N_KV_HEADS * HEAD_DIM
HG_HEADS = 8
HG_DIM = 128
HG_CHUNK = 64
HG_FINE = 32
HG_MAX_EXPONENT = 80.0
D_FF = 2816
FF_CHUNK = 256
N_FF_CHUNKS = D_FF // FF_CHUNK
SUBLANES = 8
LANES = 128
VMEM_LIMIT_BYTES = 56 * 1024 * 1024

TOKEN_TILE = 512

_NT = (((1,), (1,)), ((), ()))
_TN = (((0,), (0,)), ((), ()))


def _rmsnorm(x, gain):
    ms = jnp.mean(x * x, axis=-1, keepdims=True)
    return x * lax.rsqrt(ms + EPS) * gain


def _silu(x):
    return x * jax.nn.sigmoid(x)


def _resident(shape):
    nd = len(shape)
    return pl.BlockSpec(shape, lambda i: (0,) * nd, pipeline_mode=pl.Buffered(1))


def _params():
    return pltpu.CompilerParams(dimension_semantics=("arbitrary",), vmem_limit_bytes=VMEM_LIMIT_BYTES)


def _qkv_kernel(h_ref, gain_ref, w_ref, o_ref):
    xn = _rmsnorm(h_ref[...], gain_ref[...]).astype(BF16)
    o_ref[...] = jnp.dot(xn, w_ref[...], preferred_element_type=F32).astype(o_ref.dtype)


def _keep_half(x, keep_low):
    lane = lax.broadcasted_iota(jnp.int32, x.shape, 1)
    keep = (lane < HEAD_DIM) if keep_low else (lane >= HEAD_DIM)
    return jnp.where(keep, x, jnp.zeros_like(x))


def _swap_halves(x):
    return jnp.concatenate([x[:, HEAD_DIM:], x[:, :HEAD_DIM]], axis=1)


def _attn_kernel(q_ref, kv_ref, kvp_ref, h_ref, wo_ref, sink_ref, o_ref, kv_scr, att_scr, *, tiles_per_seq):
    i = pl.program_id(0)
    blk = ATTN_BLOCK
    kv_scr[0:blk, :] = kvp_ref[...]
    kv_scr[blk:, :] = kv_ref[...]
    seq_first = (i % tiles_per_seq) == 0

    qi = lax.broadcasted_iota(jnp.int32, (blk, blk), 0)
    ci = lax.broadcasted_iota(jnp.int32, (blk, blk), 1)
    from_prev = ci > qi
    dist = jnp.where(from_prev, qi + blk - ci, qi - ci).astype(F32)
    low = lax.broadcasted_iota(jnp.int32, (blk, LANES), 1) < HEAD_DIM
    ones_half = [jnp.where(lax.broadcasted_iota(jnp.int32, (2 * blk, LANES), 1) < HEAD_DIM, one, zero).astype(BF16)
                 for one, zero in ((1.0, 0.0), (0.0, 1.0))]
    n_pairs = N_KV_HEADS // 2

    def block_body(n, carry):
        r0 = pl.multiple_of(n * blk, blk)
        q = q_ref[pl.ds(r0, blk), :]
        kvw = kv_scr[pl.ds(r0, 2 * blk), :]
        no_prev = seq_first & (n == 0)
        hide = from_prev & no_prev
        k_pair = [kvw[:, m * LANES:(m + 1) * LANES] for m in range(n_pairs)]
        v_pair = [kvw[:, (n_pairs + m) * LANES:(n_pairs + m + 1) * LANES] for m in range(n_pairs)]
        k_swap = [_swap_halves(x) for x in k_pair]
        v_swap = [_swap_halves(x) for x in v_pair]
        vx_cache = {}

        def value_operand(hk, par):
            if (hk, par) not in vx_cache:
                src = v_pair[hk // 2] if hk % 2 == par else v_swap[hk // 2]
                vx_cache[hk, par] = jnp.concatenate([_keep_half(src, par == 0), ones_half[par]], axis=1)
            return vx_cache[hk, par]

        for pair in range(N_Q_HEADS // 2):
            hk = (2 * pair) // Q_PER_KV
            q2 = q[:, pair * LANES:(pair + 1) * LANES] * (HEAD_DIM ** -0.5)
            acc = None
            sink_terms = []
            for par in range(2):
                qh = 2 * pair + par
                slope = 2.0 ** (-8.0 * (qh + 1) / N_Q_HEADS)
                k_use = k_pair[hk // 2] if hk % 2 == par else k_swap[hk // 2]
                s2 = lax.dot_general(_keep_half(q2, par == 0), k_use, _NT,
                                     preferred_element_type=F32)
                s = jnp.where(from_prev, s2[:, :blk], s2[:, blk:]) - slope * dist
                s = jnp.where(hide, -jnp.inf, s)
                sink = sink_ref[0, qh]
                m = jnp.maximum(jnp.max(s, axis=-1, keepdims=True), sink)
                p = jnp.exp(s - m)
                p2 = jnp.concatenate([jnp.where(from_prev, p, 0.0), jnp.where(from_prev, 0.0, p)], axis=1)
                r = jnp.dot(p2.astype(BF16), value_operand(hk, par), preferred_element_type=F32)
                acc = r if acc is None else acc + r
                sink_terms.append(jnp.exp(sink - m))
            den = acc[:, LANES:] + jnp.where(low, sink_terms[0], sink_terms[1])
            att_scr[pl.ds(r0, blk), pair * LANES:(pair + 1) * LANES] = (acc[:, :LANES] / den).astype(BF16)
        return carry

    lax.fori_loop(0, TOKEN_TILE // blk, block_body, 0)
    o_ref[...] = h_ref[...] + jnp.dot(att_scr[...], wo_ref[...], preferred_element_type=F32)


def _attention_layer(h, gain, w_in, w_out, sinks, seq_len):
    n = h.shape[0]
    tm = TOKEN_TILE
    n_tiles = n // tm
    qkv = pl.pallas_call(
        _qkv_kernel,
        out_shape=jax.ShapeDtypeStruct((n, Q_COLS + KV_COLS), BF16),
        grid=(n_tiles,),
        in_specs=[pl.BlockSpec((tm, D_MODEL), lambda i: (i, 0)),
                  _resident((1, D_MODEL)),
                  _resident((D_MODEL, Q_COLS + KV_COLS))],
        out_specs=pl.BlockSpec((tm, Q_COLS + KV_COLS), lambda i: (i, 0)),
        compiler_params=_params(),
        name="attn_qkv",
    )(h, gain.reshape(1, D_MODEL), w_in.astype(BF16))

    blocks_per_tile = tm // ATTN_BLOCK
    kv_col_block = Q_COLS // KV_COLS
    return pl.pallas_call(
        functools.partial(_attn_kernel, tiles_per_seq=seq_len // tm),
        out_shape=jax.ShapeDtypeStruct((n, D_MODEL), F32),
        grid=(n_tiles,),
        in_specs=[pl.BlockSpec((tm, Q_COLS), lambda i: (i, 0)),
                  pl.BlockSpec((tm, KV_COLS), lambda i: (i, kv_col_block)),
                  pl.BlockSpec((ATTN_BLOCK, KV_COLS),
                               lambda i: (jnp.maximum(i * blocks_per_tile - 1, 0), kv_col_block)),
                  pl.BlockSpec((tm, D_MODEL), lambda i: (i, 0)),
                  _resident((Q_COLS, D_MODEL)),
                  pl.BlockSpec(memory_space=pltpu.SMEM)],
        out_specs=pl.BlockSpec((tm, D_MODEL), lambda i: (i, 0)),
        scratch_shapes=[pltpu.VMEM((tm + ATTN_BLOCK, KV_COLS), BF16),
                        pltpu.VMEM((tm, Q_COLS), BF16)],
        compiler_params=_params(),
        name="attn_core",
    )(qkv, qkv, qkv, h, w_out.astype(BF16), sinks.reshape(1, N_Q_HEADS).astype(F32))


def _row_of_group(b, group, j):
    c = b.shape[0]
    b3 = b.reshape(c // group, group, LANES)
    return jnp.broadcast_to(b3[:, j:j + 1, :], b3.shape).reshape(c, LANES)


def _side_select(q, k, half, row):
    c = q.shape[0]
    if half >= SUBLANES:
        parts = [(q if (r // half) % 2 else k)[r:r + half] for r in range(0, c, half)]
        return jnp.concatenate(parts, axis=0)
    return jnp.where(row % (2 * half) >= half, q, k)


def _hgrn_chunk(q, k, g, v, st, row, level_of, fine):
    c = HG_CHUNK
    fine_bits = fine.bit_length() - 1
    b = g
    for sh in (1, 2, 4, 8, 16, 32):
        b = b + jnp.where(row >= sh, pltpu.roll(b, sh, 0), 0.0)
    sub = row % SUBLANES

    def mid_row(half):
        if half == 1:
            return jnp.where(sub % 2 == 1, pltpu.roll(b, 1, 0), b)
        if half == 2:
            return jnp.where(sub < 4, _row_of_group(b, 8, 1), _row_of_group(b, 8, 5))
        return _row_of_group(b, 2 * half, half - 1)

    if fine == 1:
        qf, kf = q, k
    else:
        d = b - _row_of_group(b, fine, 0)
        qf, kf = q * jnp.exp(d), k * jnp.exp(-d)
    a = lax.dot_general(qf.astype(BF16), kf.astype(BF16), _NT, preferred_element_type=F32)
    a = jnp.where((level_of >= 0) & (level_of <= fine_bits), a, 0.0)
    for lvl in range(fine_bits + 1, 7):
        half = 1 << (lvl - 1)
        z = (_side_select(q, k, half, row) * jnp.exp(-jnp.abs(b - mid_row(half)))).astype(BF16)
        a_l = lax.dot_general(z, z, _NT, preferred_element_type=F32)
        a = jnp.where(level_of == lvl, a_l, a)
    o = jnp.dot(a.astype(BF16), v, preferred_element_type=F32)
    o = o + lax.dot_general((q * jnp.exp(b)).astype(BF16), st.astype(BF16), _NT, preferred_element_type=F32)
    b_last = b[c - 1:c, :]
    k_dec = (k * jnp.exp(b_last - b)).astype(BF16)
    st_new = st * jnp.exp(b_last) + lax.dot_general(v, k_dec, _TN, preferred_element_type=F32)
    return o, st_new


def _hgrn_kernel(h_ref, gain_ref, wq_ref, wf_ref, wi_ref, wg_ref, lbl_ref, ng_ref, wo_ref, o_ref,
                 q_scr, k_scr, g_scr, v_scr, gate_scr, y_scr, yb_scr, st_scr, *, layer, tiles_per_seq):
    i = pl.program_id(0)
    x = h_ref[...]
    xn = _rmsnorm(x, gain_ref[...]).astype(BF16)

    logits = lbl_ref[...]
    ex = jnp.exp(logits - jnp.max(logits, axis=0, keepdims=True))
    sm = ex / jnp.sum(ex, axis=0, keepdims=True)
    cum = sm[0:1, :]
    for r in range(1, layer + 1):
        cum = cum + sm[r:r + 1, :]
    lb = cum - sm[0:1, :]
    log_lb = jnp.log(lb)
    log_1m = jnp.log1p(-lb)

    q_scr[...] = jnp.dot(xn, wq_ref[...], preferred_element_type=F32)
    g_scr[...] = jnp.dot(xn, wf_ref[...], preferred_element_type=F32)
    q_scr[...] = _silu(q_scr[...])

    def gates_and_rest(lb_bounded):
        vp = jnp.dot(xn, wi_ref[...], preferred_element_type=F32)
        f = g_scr[...]
        if lb_bounded:
            t = jnp.exp(-jnp.abs(f))
            r = 1.0 / (1.0 + t)
            tr = t * r
            sig = jnp.where(f >= 0.0, r, tr)
            g_scr[...] = jnp.log(lb + (1.0 - lb) * sig)
            k_scr[...] = (1.0 - lb) * jnp.where(f >= 0.0, tr, r)
        else:
            log_sig = jnp.minimum(f, 0.0) - jnp.log1p(jnp.exp(-jnp.abs(f)))
            u = log_1m + log_sig
            g_scr[...] = jnp.maximum(log_lb, u) + jnp.log1p(jnp.exp(-jnp.abs(log_lb - u)))
            k_scr[...] = (1.0 - lb) * jax.nn.sigmoid(-f)
        gate_scr[...] = jnp.dot(xn, wg_ref[...], preferred_element_type=F32)
        v_scr[...] = vp.astype(BF16)

    @pl.when((i % tiles_per_seq) == 0)
    def _():
        st_scr[...] = jnp.zeros_like(st_scr)

    c = HG_CHUNK
    row = lax.broadcasted_iota(jnp.int32, (c, LANES), 0)
    ti = lax.broadcasted_iota(jnp.int32, (c, c), 0)
    si = lax.broadcasted_iota(jnp.int32, (c, c), 1)
    diff = ti ^ si
    level_of = sum((diff >= (1 << bit)).astype(jnp.int32) for bit in range(6))
    level_of = jnp.where(si > ti, -1, level_of)

    def run_chunks(fine):
        def chunk_body(ci, carry):
            r0 = pl.multiple_of(ci * c, c)
            for hh in range(HG_HEADS):
                cs = slice(hh * HG_DIM, (hh + 1) * HG_DIM)
                o, st_new = _hgrn_chunk(q_scr[pl.ds(r0, c), cs], k_scr[pl.ds(r0, c), cs],
                                        g_scr[pl.ds(r0, c), cs], v_scr[pl.ds(r0, c), cs], st_scr[hh],
                                        row, level_of, fine)
                st_scr[hh] = st_new
                y_scr[pl.ds(r0, c), cs] = o
            return carry
        lax.fori_loop(0, TOKEN_TILE // c, chunk_body, 0)

    blocked_ok = (HG_FINE - 1) * jnp.max(-log_lb) <= HG_MAX_EXPONENT

    @pl.when(blocked_ok)
    def _():
        gates_and_rest(True)
        run_chunks(HG_FINE)

    @pl.when(jnp.logical_not(blocked_ok))
    def _():
        gates_and_rest(False)
        run_chunks(1)

    for hh in range(HG_HEADS):
        cs = slice(hh * HG_DIM, (hh + 1) * HG_DIM)
        yb_scr[:, cs] = (_rmsnorm(y_scr[:, cs], ng_ref[...]) * _silu(gate_scr[:, cs])).astype(BF16)
    o_ref[...] = x + jnp.dot(yb_scr[...], wo_ref[...], preferred_element_type=F32)


def _hgrn_layer(h, gain, w_in, w_out, norm_g, lb_logits, layer, seq_len):
    n = h.shape[0]
    tm = TOKEN_TILE
    d = D_MODEL
    w = w_in.astype(BF16)
    tile = pl.BlockSpec((tm, d), lambda i: (i, 0))
    return pl.pallas_call(
        functools.partial(_hgrn_kernel, layer=layer, tiles_per_seq=seq_len // tm),
        out_shape=jax.ShapeDtypeStruct((n, d), F32),
        grid=(n // tm,),
        in_specs=[tile, _resident((1, d)),
                  _resident((d, d)), _resident((d, d)), _resident((d, d)), _resident((d, d)),
                  _resident((DEPTH, d)), _resident((1, HG_DIM)), _resident((d, d))],
        out_specs=tile,
        scratch_shapes=[pltpu.VMEM((tm, d), F32), pltpu.VMEM((tm, d), F32), pltpu.VMEM((tm, d), F32),
                        pltpu.VMEM((tm, d), BF16), pltpu.VMEM((tm, d), F32), pltpu.VMEM((tm, d), F32),
                        pltpu.VMEM((tm, d), BF16), pltpu.VMEM((HG_HEADS, HG_DIM, HG_DIM), F32)],
        compiler_params=_params(),
        name=f"hgrn_l{layer}",
    )(h, gain.reshape(1, d), w[:, 0:d], w[:, d:2 * d], w[:, 2 * d:3 * d], w[:, 3 * d:4 * d],
      lb_logits.astype(F32), norm_g.reshape(1, HG_DIM), w_out.astype(BF16))


def _ffn_kernel(h_ref, gain_ref, wu_ref, cw_ref, cb_ref, wd_ref, gfin_ref, o_ref,
                xn_scr, acc_scr, u_scr, halo_scr, *, tiles_per_seq, final_norm):
    i = pl.program_id(0)
    tm = TOKEN_TILE
    hr = SUBLANES
    n_lane_tiles = FF_CHUNK // LANES
    x = h_ref[...]
    xn_scr[...] = _rmsnorm(x, gain_ref[...]).astype(BF16)
    seq_first = (i % tiles_per_seq) == 0

    def up_proj(j):
        xn = xn_scr[...]
        for part in range(2):
            col0 = part * D_FF + j * FF_CHUNK
            u = jnp.dot(xn, wu_ref[:, col0:col0 + FF_CHUNK], preferred_element_type=F32)
            for c in range(n_lane_tiles):
                u_scr[j % 2, part, c, hr:, :] = u[:, c * LANES:(c + 1) * LANES]

    def conv(j, part):
        cols = []
        for c in range(n_lane_tiles):
            us = u_scr.at[j % 2, part, c]
            us[0:hr, :] = jnp.where(seq_first, 0.0, halo_scr[j, part, c])
            halo_scr[j, part, c] = us[tm:tm + hr, :]
            col0 = part * D_FF + j * FF_CHUNK + c * LANES
            lanes = slice(col0, col0 + LANES)
            cols.append(cb_ref[:, lanes] + cw_ref[0:1, lanes] * us[hr - 2:hr - 2 + tm, :]
                        + cw_ref[1:2, lanes] * us[hr - 1:hr - 1 + tm, :] + cw_ref[2:3, lanes] * us[hr:hr + tm, :])
        return jnp.concatenate(cols, axis=-1)

    up_proj(0)
    for j in range(N_FF_CHUNKS):
        if j + 1 < N_FF_CHUNKS:
            up_proj(j + 1)
        act = (_silu(conv(j, 0)) * conv(j, 1)).astype(BF16)
        dn = jnp.dot(act, wd_ref[j * FF_CHUNK:(j + 1) * FF_CHUNK, :], preferred_element_type=F32)
        if j == 0:
            acc_scr[...] = dn
        else:
            acc_scr[...] += dn
    y = x + acc_scr[...]
    if final_norm:
        y = _rmsnorm(y, gfin_ref[...])
    o_ref[...] = y


def _ffn_layer(h, gain, w_up, conv_w, conv_b, w_down, gain_final, final_norm, seq_len):
    n = h.shape[0]
    tm = TOKEN_TILE
    d = D_MODEL
    tile = pl.BlockSpec((tm, d), lambda i: (i, 0))
    nc, fc = N_FF_CHUNKS, FF_CHUNK
    return pl.pallas_call(
        functools.partial(_ffn_kernel, tiles_per_seq=seq_len // tm, final_norm=final_norm),
        out_shape=jax.ShapeDtypeStruct((n, d), F32),
        grid=(n // tm,),
        in_specs=[tile, _resident((1, d)), _resident((d, 2 * D_FF)), _resident((3, 2 * D_FF)),
                  _resident((1, 2 * D_FF)), _resident((D_FF, d)), _resident((1, d))],
        out_specs=tile,
        scratch_shapes=[pltpu.VMEM((tm, d), BF16), pltpu.VMEM((tm, d), F32),
                        pltpu.VMEM((2, 2, fc // LANES, tm + SUBLANES, LANES), F32),
                        pltpu.VMEM((nc, 2, fc // LANES, SUBLANES, LANES), F32)],
        compiler_params=_params(),
        name="conv_ffn",
    )(h, gain.reshape(1, d), w_up.astype(BF16), conv_w, conv_b.reshape(1, 2 * D_FF), w_down.astype(BF16),
      gain_final.reshape(1, d))


def kernel(x, norm_mix, norm_ffn, norm_final, attn_w_in, attn_w_out, attn_sinks, hgrn_w_in, hgrn_w_out,
           hgrn_norm, hgrn_lb_logits, ffn_w_up, ffn_conv_w, ffn_conv_b, ffn_w_down):
    bsz, seq_len, d = x.shape
    assert d == D_MODEL and seq_len % TOKEN_TILE == 0
    h = x.reshape(bsz * seq_len, d)
    for layer in range(DEPTH):
        idx = layer // 2
        if layer % 2 == 0:
            h = _attention_layer(h, norm_mix[layer], attn_w_in[idx], attn_w_out[idx], attn_sinks[idx], seq_len)
        else:
            h = _hgrn_layer(h, norm_mix[layer], hgrn_w_in[idx], hgrn_w_out[idx], hgrn_norm[idx],
                            hgrn_lb_logits, layer, seq_len)
        h = _ffn_layer(h, norm_ffn[layer], ffn_w_up[layer], ffn_conv_w[layer], ffn_conv_b[layer],
                       ffn_w_down[layer], norm_final, layer == DEPTH - 1, seq_len)
    return h.reshape(bsz, seq_len, d)
```

```python
import functools

import jax
import jax.numpy as jnp
from jax import lax
from jax.experimental import pallas as pl
from jax.experimental.pallas import tpu as pltpu

F32 = jnp.float32
BF16 = jnp.bfloat16

D_MODEL = 1024
DEPTH = 4
EPS = 1e-6
HEAD_DIM = 64
N_Q_HEADS = 16
N_KV_HEADS = 4
Q_PER_KV = N_Q_HEADS // N_KV_HEADS
ATTN_BLOCK = 128
Q_COLS = N_Q_HEADS * HEAD_DIM
KV_COLS = 2 * N_KV_HEADS * HEAD_DIM
HG_HEADS = 8
HG_DIM = 128
HG_CHUNK = 64
HG_FINE = 32
HG_MAX_EXPONENT = 80.0
D_FF = 2816
FF_CHUNK = 256
N_FF_CHUNKS = D_FF // FF_CHUNK
SUBLANES = 8
LANES = 128
VMEM_LIMIT_BYTES = 56 * 1024 * 1024

TOKEN_TILE = 512

_NT = (((1,), (1,)), ((), ()))
_TN = (((0,), (0,)), ((), ()))


def _rmsnorm(x, gain):
    ms = jnp.mean(x * x, axis=-1, keepdims=True)
    return x * lax.rsqrt(ms + EPS) * gain


def _silu(x):
    return x * jax.nn.sigmoid(x)


def _resident(shape):
    nd = len(shape)
    return pl.BlockSpec(shape, lambda i: (0,) * nd, pipeline_mode=pl.Buffered(1))


def _params():
    return pltpu.CompilerParams(dimension_semantics=("arbitrary",), vmem_limit_bytes=VMEM_LIMIT_BYTES)


def _qkv_kernel(h_ref, gain_ref, w_ref, o_ref):
    xn = _rmsnorm(h_ref[...], gain_ref[...]).astype(BF16)
    o_ref[...] = jnp.dot(xn, w_ref[...], preferred_element_type=F32).astype(o_ref.dtype)


def _keep_half(x, keep_low):
    lane = lax.broadcasted_iota(jnp.int32, x.shape, 1)
    keep = (lane < HEAD_DIM) if keep_low else (lane >= HEAD_DIM)
    return jnp.where(keep, x, jnp.zeros_like(x))


def _swap_halves(x):
    return jnp.concatenate([x[:, HEAD_DIM:], x[:, :HEAD_DIM]], axis=1)


def _attn_kernel(q_ref, kv_ref, kvp_ref, h_ref, wo_ref, sink_ref, o_ref, kv_scr, att_scr, *, tiles_per_seq):
    i = pl.program_id(0)
    blk = ATTN_BLOCK
    kv_scr[0:blk, :] = kvp_ref[...]
    kv_scr[blk:, :] = kv_ref[...]
    seq_first = (i % tiles_per_seq) == 0

    qi = lax.broadcasted_iota(jnp.int32, (blk, blk), 0)
    ci = lax.broadcasted_iota(jnp.int32, (blk, blk), 1)
    from_prev = ci > qi
    dist = jnp.where(from_prev, qi + blk - ci, qi - ci).astype(F32)
    low = lax.broadcasted_iota(jnp.int32, (blk, LANES), 1) < HEAD_DIM
    ones_half = [jnp.where(lax.broadcasted_iota(jnp.int32, (2 * blk, LANES), 1) < HEAD_DIM, one, zero).astype(BF16)
                 for one, zero in ((1.0, 0.0), (0.0, 1.0))]
    n_pairs = N_KV_HEADS // 2

    def block_body(n, carry):
        r0 = n * blk
        q = q_ref[pl.ds(r0, blk), :]
        kvw = kv_scr[pl.ds(r0, 2 * blk), :]
        hide = from_prev & seq_first
        k_pair = [kvw[:, m * LANES:(m + 1) * LANES] for m in range(n_pairs)]
        v_pair = [kvw[:, (n_pairs + m) * LANES:(n_pairs + m + 1) * LANES] for m in range(n_pairs)]
        k_swap = [_swap_halves(x) for x in k_pair]
        v_swap = [_swap_halves(x) for x in v_pair]
        vx_cache = {}

        def value_operand(hk, par):
            if (hk, par) not in vx_cache:
                src = v_pair[hk // 2] if hk % 2 == par else v_swap[hk // 2]
                vx_cache[hk, par] = jnp.concatenate([_keep_half(src, par == 0), ones_half[par]], axis=1)
            return vx_cache[hk, par]

        for pair in range(N_Q_HEADS // 2):
            hk = (2 * pair) // Q_PER_KV
            q2 = q[:, pair * LANES:(pair + 1) * LANES] * (HEAD_DIM ** -0.5)
            acc = None
            sink_terms = []
            for par in range(2):
                qh = 2 * pair + par
                slope = 2.0 ** (-8.0 * (qh + 1) / N_Q_HEADS)
                k_use = k_pair[hk // 2] if hk % 2 == par else k_swap[hk // 2]
                s2 = lax.dot_general(_keep_half(q2, par == 0), k_use, _NT,
                                     preferred_element_type=F32)
                s = jnp.where(from_prev, s2[:, :blk], s2[:, blk:]) - slope * dist
                if n == 0:
                    s = jnp.where(hide, -jnp.inf, s)
                sink = sink_ref[0, qh]
                m = jnp.maximum(jnp.max(s, axis=-1, keepdims=True), sink)
                p = jnp.exp(s - m)
                p2 = jnp.concatenate([jnp.where(from_prev, p, 0.0), jnp.where(from_prev, 0.0, p)], axis=1)
                r = jnp.dot(p2.astype(BF16), value_operand(hk, par), preferred_element_type=F32)
                acc = r if acc is None else acc + r
                sink_terms.append(jnp.exp(sink - m))
            den = acc[:, LANES:] + jnp.where(low, sink_terms[0], sink_terms[1])
            att_scr[pl.ds(r0, blk), pair * LANES:(pair + 1) * LANES] = (acc[:, :LANES] / den).astype(BF16)
        return carry

    for n in range(TOKEN_TILE // blk):
        block_body(n, 0)
    o_ref[...] = h_ref[...] + jnp.dot(att_scr[...], wo_ref[...], preferred_element_type=F32)


def _attention_layer(h, gain, w_in, w_out, sinks, seq_len):
    n = h.shape[0]
    tm = TOKEN_TILE
    n_tiles = n // tm
    qkv = pl.pallas_call(
        _qkv_kernel,
        out_shape=jax.ShapeDtypeStruct((n, Q_COLS + KV_COLS), BF16),
        grid=(n_tiles,),
        in_specs=[pl.BlockSpec((tm, D_MODEL), lambda i: (i, 0)),
                  _resident((1, D_MODEL)),
                  _resident((D_MODEL, Q_COLS + KV_COLS))],
        out_specs=pl.BlockSpec((tm, Q_COLS + KV_COLS), lambda i: (i, 0)),
        compiler_params=_params(),
        name="attn_qkv",
    )(h, gain.reshape(1, D_MODEL), w_in.astype(BF16))

    blocks_per_tile = tm // ATTN_BLOCK
    kv_col_block = Q_COLS // KV_COLS
    return pl.pallas_call(
        functools.partial(_attn_kernel, tiles_per_seq=seq_len // tm),
        out_shape=jax.ShapeDtypeStruct((n, D_MODEL), F32),
        grid=(n_tiles,),
        in_specs=[pl.BlockSpec((tm, Q_COLS), lambda i: (i, 0)),
                  pl.BlockSpec((tm, KV_COLS), lambda i: (i, kv_col_block)),
                  pl.BlockSpec((ATTN_BLOCK, KV_COLS),
                               lambda i: (jnp.maximum(i * blocks_per_tile - 1, 0), kv_col_block)),
                  pl.BlockSpec((tm, D_MODEL), lambda i: (i, 0)),
                  _resident((Q_COLS, D_MODEL)),
                  pl.BlockSpec(memory_space=pltpu.SMEM)],
        out_specs=pl.BlockSpec((tm, D_MODEL), lambda i: (i, 0)),
        scratch_shapes=[pltpu.VMEM((tm + ATTN_BLOCK, KV_COLS), BF16),
                        pltpu.VMEM((tm, Q_COLS), BF16)],
        compiler_params=_params(),
        name="attn_core",
    )(qkv, qkv, qkv, h, w_out.astype(BF16), sinks.reshape(1, N_Q_HEADS).astype(F32))


def _row_of_group(b, group, j):
    c = b.shape[0]
    b3 = b.reshape(c // group, group, LANES)
    return jnp.broadcast_to(b3[:, j:j + 1, :], b3.shape).reshape(c, LANES)


def _side_select(q, k, half, row):
    c = q.shape[0]
    if half >= SUBLANES:
        parts = [(q if (r // half) % 2 else k)[r:r + half] for r in range(0, c, half)]
        return jnp.concatenate(parts, axis=0)
    return jnp.where(row % (2 * half) >= half, q, k)


def _prefix_rows(g, row):
    b = g
    for sh in (1, 2, 4, 8, 16, 32):
        b = b + jnp.where(row >= sh, pltpu.roll(b, sh, 0), 0.0)
    return b


def _hgrn_chunk(q, k, g, v, st, row, level_of):
    c = HG_CHUNK
    b = _prefix_rows(g, row)
    sub = row % SUBLANES

    def mid_row(half):
        if half == 1:
            return jnp.where(sub % 2 == 1, pltpu.roll(b, 1, 0), b)
        if half == 2:
            return jnp.where(sub < 4, _row_of_group(b, 8, 1), _row_of_group(b, 8, 5))
        return _row_of_group(b, 2 * half, half - 1)

    a = lax.dot_general(q.astype(BF16), k.astype(BF16), _NT, preferred_element_type=F32)
    a = jnp.where(level_of == 0, a, 0.0)
    for lvl in range(1, 7):
        half = 1 << (lvl - 1)
        z = (_side_select(q, k, half, row) * jnp.exp(-jnp.abs(b - mid_row(half)))).astype(BF16)
        a = jnp.where(level_of == lvl, lax.dot_general(z, z, _NT, preferred_element_type=F32), a)
    o = jnp.dot(a.astype(BF16), v, preferred_element_type=F32)
    o = o + lax.dot_general((q * jnp.exp(b)).astype(BF16), st.astype(BF16), _NT, preferred_element_type=F32)
    b_last = b[c - 1:c, :]
    k_dec = (k * jnp.exp(b_last - b)).astype(BF16)
    st_new = st * jnp.exp(b_last) + lax.dot_general(v, k_dec, _TN, preferred_element_type=F32)
    return o, st_new


def _hgrn_kernel(h_ref, gain_ref, wq_ref, wf_ref, wi_ref, wg_ref, lbl_ref, ng_ref, wo_ref, o_ref,
                 xn_scr, q_scr, k_scr, g_scr, v_scr, gate_scr, y_scr, yb_scr, st_scr,
                 *, layer, tiles_per_seq):
    i = pl.program_id(0)
    c = HG_CHUNK
    n_chunks = TOKEN_TILE // c
    x = h_ref[...]
    xn_scr[...] = _rmsnorm(x, gain_ref[...]).astype(BF16)

    logits = lbl_ref[...]
    ex = jnp.exp(logits - jnp.max(logits, axis=0, keepdims=True))
    sm = ex / jnp.sum(ex, axis=0, keepdims=True)
    cum = sm[0:1, :]
    for r in range(1, layer + 1):
        cum = cum + sm[r:r + 1, :]
    lb = cum - sm[0:1, :]
    log_lb = jnp.log(lb)
    log_1m = jnp.log1p(-lb)

    @pl.when((i % tiles_per_seq) == 0)
    def _():
        st_scr[...] = jnp.zeros_like(st_scr)

    row = lax.broadcasted_iota(jnp.int32, (c, LANES), 0)
    ti = lax.broadcasted_iota(jnp.int32, (c, c), 0)
    si = lax.broadcasted_iota(jnp.int32, (c, c), 1)
    diff = ti ^ si
    level_of = sum((diff >= (1 << bit)).astype(jnp.int32) for bit in range(6))
    level_of = jnp.where(si > ti, -1, level_of)
    fine_bits = HG_FINE.bit_length() - 1
    in_fine_block = (level_of >= 0) & (level_of <= fine_bits)
    across_blocks = level_of == fine_bits + 1

    def project(cols):
        xn = xn_scr[...]
        g_scr[:, cols] = jnp.dot(xn, wf_ref[:, cols], preferred_element_type=F32)
        q_scr[:, cols] = jnp.dot(xn, wq_ref[:, cols], preferred_element_type=F32)
        v_scr[:, cols] = jnp.dot(xn, wi_ref[:, cols], preferred_element_type=F32).astype(BF16)
        gate_scr[:, cols] = jnp.dot(xn, wg_ref[:, cols], preferred_element_type=F32)

    blocked_ok = (HG_FINE - 1) * jnp.max(-log_lb) <= HG_MAX_EXPONENT

    @pl.when(blocked_ok)
    def _():
        def tile_work(ci, hh):
            rows = slice(ci * c, (ci + 1) * c)
            cs = slice(hh * HG_DIM, (hh + 1) * HG_DIM)
            lbh = lb[:, cs]
            f = g_scr[rows, cs]
            q = _silu(q_scr[rows, cs])
            t = jnp.exp(-jnp.abs(f))
            r = 1.0 / (1.0 + t)
            tr = t * r
            g = jnp.log(lbh + (1.0 - lbh) * jnp.where(f >= 0.0, r, tr))
            k = (1.0 - lbh) * jnp.where(f >= 0.0, tr, r)
            bsum = _prefix_rows(g, row)
            d = bsum - _row_of_group(bsum, HG_FINE, 0)
            a = lax.dot_general((q * jnp.exp(d)).astype(BF16), (k * jnp.exp(-d)).astype(BF16), _NT,
                                preferred_element_type=F32)
            mid = _row_of_group(bsum, 2 * HG_FINE, HG_FINE - 1)
            z = (_side_select(q, k, HG_FINE, row) * jnp.exp(-jnp.abs(bsum - mid))).astype(BF16)
            a_z = lax.dot_general(z, z, _NT, preferred_element_type=F32)
            v = v_scr[rows, cs]
            b_last = bsum[c - 1:c, :]
            k_dec = (k * jnp.exp(b_last - bsum)).astype(BF16)
            u = lax.dot_general(v, k_dec, _TN, preferred_element_type=F32)
            return dict(a=a, a_z=a_z, v=v, u=u, qe=(q * jnp.exp(bsum)).astype(BF16), ebl=jnp.exp(b_last))

        def tile_output(ci, hh, t, inter):
            rows = slice(ci * c, (ci + 1) * c)
            cs = slice(hh * HG_DIM, (hh + 1) * HG_DIM)
            a = jnp.where(across_blocks, t["a_z"], jnp.where(in_fine_block, t["a"], 0.0))
            y_scr[rows, cs] = jnp.dot(a.astype(BF16), t["v"], preferred_element_type=F32) + inter

        pair_cols = 2 * HG_DIM
        n_pairs = HG_HEADS // 2
        project(slice(0, pair_cols))
        for p in range(n_pairs):
            if p + 1 < n_pairs:
                project(slice((p + 1) * pair_cols, (p + 2) * pair_cols))
            heads = (2 * p, 2 * p + 1)
            tiles = {(ci, hh): tile_work(ci, hh) for ci in range(n_chunks) for hh in heads}
            inter = {}
            for hh in heads:
                st = st_scr[hh]
                for ci in range(n_chunks):
                    t = tiles[ci, hh]
                    inter[ci, hh] = lax.dot_general(t["qe"], st.astype(BF16), _NT, preferred_element_type=F32)
                    st = st * t["ebl"] + t["u"]
                st_scr[hh] = st
            for key, t in tiles.items():
                tile_output(*key, t, inter[key])

    @pl.when(jnp.logical_not(blocked_ok))
    def _():
        project(slice(0, D_MODEL))
        q_scr[...] = _silu(q_scr[...])
        f = g_scr[...]
        log_sig = jnp.minimum(f, 0.0) - jnp.log1p(jnp.exp(-jnp.abs(f)))
        u = log_1m + log_sig
        g_scr[...] = jnp.maximum(log_lb, u) + jnp.log1p(jnp.exp(-jnp.abs(log_lb - u)))
        k_scr[...] = (1.0 - lb) * jax.nn.sigmoid(-f)

        def chunk_body(ci, carry):
            r0 = pl.multiple_of(ci * c, c)
            for hh in range(HG_HEADS):
                cs = slice(hh * HG_DIM, (hh + 1) * HG_DIM)
                o, st_new = _hgrn_chunk(q_scr[pl.ds(r0, c), cs], k_scr[pl.ds(r0, c), cs],
                                        g_scr[pl.ds(r0, c), cs], v_scr[pl.ds(r0, c), cs], st_scr[hh],
                                        row, level_of)
                st_scr[hh] = st_new
                y_scr[pl.ds(r0, c), cs] = o
            return carry
        lax.fori_loop(0, n_chunks, chunk_body, 0)

    for hh in range(HG_HEADS):
        cs = slice(hh * HG_DIM, (hh + 1) * HG_DIM)
        yb_scr[:, cs] = (_rmsnorm(y_scr[:, cs], ng_ref[...]) * _silu(gate_scr[:, cs])).astype(BF16)
    o_ref[...] = x + jnp.dot(yb_scr[...], wo_ref[...], preferred_element_type=F32)


def _hgrn_layer(h, gain, w_in, w_out, norm_g, lb_logits, layer, seq_len):
    assert 2 * HG_FINE == HG_CHUNK
    n = h.shape[0]
    tm = TOKEN_TILE
    d = D_MODEL
    w = w_in.astype(BF16)
    tile = pl.BlockSpec((tm, d), lambda i: (i, 0))
    f32_tile = pltpu.VMEM((tm, d), F32)
    bf16_tile = pltpu.VMEM((tm, d), BF16)
    return pl.pallas_call(
        functools.partial(_hgrn_kernel, layer=layer, tiles_per_seq=seq_len // tm),
        out_shape=jax.ShapeDtypeStruct((n, d), F32),
        grid=(n // tm,),
        in_specs=[tile, _resident((1, d)),
                  _resident((d, d)), _resident((d, d)), _resident((d, d)), _resident((d, d)),
                  _resident((DEPTH, d)), _resident((1, HG_DIM)), _resident((d, d))],
        out_specs=tile,
        scratch_shapes=[bf16_tile, f32_tile, f32_tile, f32_tile, bf16_tile, f32_tile, f32_tile, bf16_tile,
                        pltpu.VMEM((HG_HEADS, HG_DIM, HG_DIM), F32)],
        compiler_params=_params(),
        name=f"hgrn_l{layer}",
    )(h, gain.reshape(1, d), w[:, 0:d], w[:, d:2 * d], w[:, 2 * d:3 * d], w[:, 3 * d:4 * d],
      lb_logits.astype(F32), norm_g.reshape(1, HG_DIM), w_out.astype(BF16))


def _ffn_kernel(h_ref, gain_ref, wu_ref, cw_ref, cb_ref, wd_ref, gfin_ref, o_ref,
                xn_scr, acc_scr, u_scr, halo_scr, *, tiles_per_seq, final_norm):
    i = pl.program_id(0)
    tm = TOKEN_TILE
    hr = SUBLANES
    n_lane_tiles = FF_CHUNK // LANES
    x = h_ref[...]
    xn_scr[...] = _rmsnorm(x, gain_ref[...]).astype(BF16)
    seq_first = (i % tiles_per_seq) == 0

    def up_proj(j):
        xn = xn_scr[...]
        for part in range(2):
            col0 = part * D_FF + j * FF_CHUNK
            u = jnp.dot(xn, wu_ref[:, col0:col0 + FF_CHUNK], preferred_element_type=F32)
            for c in range(n_lane_tiles):
                u_scr[j % 2, part, c, hr:, :] = u[:, c * LANES:(c + 1) * LANES]

    def conv(j, part):
        cols = []
        for c in range(n_lane_tiles):
            us = u_scr.at[j % 2, part, c]
            us[0:hr, :] = jnp.where(seq_first, 0.0, halo_scr[j, part, c])
            halo_scr[j, part, c] = us[tm:tm + hr, :]
            col0 = part * D_FF + j * FF_CHUNK + c * LANES
            lanes = slice(col0, col0 + LANES)
            cols.append(cb_ref[:, lanes] + cw_ref[0:1, lanes] * us[hr - 2:hr - 2 + tm, :]
                        + cw_ref[1:2, lanes] * us[hr - 1:hr - 1 + tm, :] + cw_ref[2:3, lanes] * us[hr:hr + tm, :])
        return jnp.concatenate(cols, axis=-1)

    up_proj(0)
    for j in range(N_FF_CHUNKS):
        if j + 1 < N_FF_CHUNKS:
            up_proj(j + 1)
        act = (_silu(conv(j, 0)) * conv(j, 1)).astype(BF16)
        dn = jnp.dot(act, wd_ref[j * FF_CHUNK:(j + 1) * FF_CHUNK, :], preferred_element_type=F32)
        if j == 0:
            acc_scr[...] = dn
        else:
            acc_scr[...] += dn
    y = x + acc_scr[...]
    if final_norm:
        y = _rmsnorm(y, gfin_ref[...])
    o_ref[...] = y


def _ffn_layer(h, gain, w_up, conv_w, conv_b, w_down, gain_final, final_norm, seq_len):
    n = h.shape[0]
    tm = TOKEN_TILE
    d = D_MODEL
    tile = pl.BlockSpec((tm, d), lambda i: (i, 0))
    nc, fc = N_FF_CHUNKS, FF_CHUNK
    return pl.pallas_call(
        functools.partial(_ffn_kernel, tiles_per_seq=seq_len // tm, final_norm=final_norm),
        out_shape=jax.ShapeDtypeStruct((n, d), F32),
        grid=(n // tm,),
        in_specs=[tile, _resident((1, d)), _resident((d, 2 * D_FF)), _resident((3, 2 * D_FF)),
                  _resident((1, 2 * D_FF)), _resident((D_FF, d)), _resident((1, d))],
        out_specs=tile,
        scratch_shapes=[pltpu.VMEM((tm, d), BF16), pltpu.VMEM((tm, d), F32),
                        pltpu.VMEM((2, 2, fc // LANES, tm + SUBLANES, LANES), F32),
                        pltpu.VMEM((nc, 2, fc // LANES, SUBLANES, LANES), F32)],
        compiler_params=_params(),
        name="conv_ffn",
    )(h, gain.reshape(1, d), w_up.astype(BF16), conv_w, conv_b.reshape(1, 2 * D_FF), w_down.astype(BF16),
      gain_final.reshape(1, d))


def kernel(x, norm_mix, norm_ffn, norm_final, attn_w_in, attn_w_out, attn_sinks, hgrn_w_in, hgrn_w_out,
           hgrn_norm, hgrn_lb_logits, ffn_w_up, ffn_conv_w, ffn_conv_b, ffn_w_down):
    bsz, seq_len, d = x.shape
    assert d == D_MODEL and seq_len % TOKEN_TILE == 0
    h = x.reshape(bsz * seq_len, d)
    for layer in range(DEPTH):
        idx = layer // 2
        if layer % 2 == 0:
            h = _attention_layer(h, norm_mix[layer], attn_w_in[idx], attn_w_out[idx], attn_sinks[idx], seq_len)
        else:
            h = _hgrn_layer(h, norm_mix[layer], hgrn_w_in[idx], hgrn_w_out[idx], hgrn_norm[idx],
                            hgrn_lb_logits, layer, seq_len)
        h = _ffn_layer(h, norm_ffn[layer], ffn_w_up[layer], ffn_conv_w[layer], ffn_conv_b[layer],
                       ffn_w_down[layer], norm_final, layer == DEPTH - 1, seq_len)
    return h.reshape(bsz, seq_len, d)
```

```python
import functools

import jax
import jax.numpy as jnp
from jax import lax
from jax.experimental import pallas as pl
from jax.experimental.pallas import tpu as pltpu

F32 = jnp.float32
BF16 = jnp.bfloat16

D_MODEL = 1024
DEPTH = 4
EPS = 1e-6
HEAD_DIM = 64
N_Q_HEADS = 16
N_KV_HEADS = 4
Q_PER_KV = N_Q_HEADS // N_KV_HEADS
ATTN_BLOCK = 128
Q_COLS = N_Q_HEADS * HEAD_DIM
KV_COLS = 2 * N_KV_HEADS * HEAD_DIM
HG_HEADS = 8
HG_DIM = 128
HG_CHUNK = 64
HG_FINE = 32
HG_MAX_EXPONENT = 80.0
D_FF = 2816
FF_CHUNK = 256
N_FF_CHUNKS = D_FF // FF_CHUNK
SUBLANES = 8
LANES = 128
VMEM_LIMIT_BYTES = 56 * 1024 * 1024

TOKEN_TILE = 512

_NT = (((1,), (1,)), ((), ()))
_TN = (((0,), (0,)), ((), ()))


def _rmsnorm(x, gain):
    ms = jnp.mean(x * x, axis=-1, keepdims=True)
    return x * lax.rsqrt(ms + EPS) * gain


def _silu(x):
    return x * jax.nn.sigmoid(x)


def _resident(shape):
    nd = len(shape)
    return pl.BlockSpec(shape, lambda i: (0,) * nd, pipeline_mode=pl.Buffered(1))


def _params():
    return pltpu.CompilerParams(dimension_semantics=("arbitrary",), vmem_limit_bytes=VMEM_LIMIT_BYTES)


def _qkv_kernel(h_ref, gain_ref, w_ref, o_ref):
    xn = _rmsnorm(h_ref[...], gain_ref[...]).astype(BF16)
    o_ref[...] = jnp.dot(xn, w_ref[...], preferred_element_type=F32).astype(o_ref.dtype)


def _keep_half(x, keep_low):
    lane = lax.broadcasted_iota(jnp.int32, x.shape, 1)
    keep = (lane < HEAD_DIM) if keep_low else (lane >= HEAD_DIM)
    return jnp.where(keep, x, jnp.zeros_like(x))


def _swap_halves(x):
    return jnp.concatenate([x[:, HEAD_DIM:], x[:, :HEAD_DIM]], axis=1)


def _attn_kernel(q_ref, kv_ref, kvp_ref, h_ref, wo_ref, sink_ref, o_ref, kv_scr, att_scr, *, tiles_per_seq):
    i = pl.program_id(0)
    blk = ATTN_BLOCK
    kv_scr[0:blk, :] = kvp_ref[...]
    kv_scr[blk:, :] = kv_ref[...]
    seq_first = (i % tiles_per_seq) == 0

    qi = lax.broadcasted_iota(jnp.int32, (blk, blk), 0)
    ci = lax.broadcasted_iota(jnp.int32, (blk, blk), 1)
    from_prev = ci > qi
    dist = jnp.where(from_prev, qi + blk - ci, qi - ci).astype(F32)
    low = lax.broadcasted_iota(jnp.int32, (blk, LANES), 1) < HEAD_DIM
    ones_half = [jnp.where(lax.broadcasted_iota(jnp.int32, (2 * blk, LANES), 1) < HEAD_DIM, one, zero).astype(BF16)
                 for one, zero in ((1.0, 0.0), (0.0, 1.0))]
    n_pairs = N_KV_HEADS // 2

    def block_body(n, carry):
        r0 = n * blk
        q = q_ref[pl.ds(r0, blk), :]
        kvw = kv_scr[pl.ds(r0, 2 * blk), :]
        hide = from_prev & seq_first
        k_pair = [kvw[:, m * LANES:(m + 1) * LANES] for m in range(n_pairs)]
        v_pair = [kvw[:, (n_pairs + m) * LANES:(n_pairs + m + 1) * LANES] for m in range(n_pairs)]
        k_swap = [_swap_halves(x) for x in k_pair]
        v_swap = [_swap_halves(x) for x in v_pair]
        vx_cache = {}

        def value_operand(hk, par):
            if (hk, par) not in vx_cache:
                src = v_pair[hk // 2] if hk % 2 == par else v_swap[hk // 2]
                vx_cache[hk, par] = jnp.concatenate([_keep_half(src, par == 0), ones_half[par]], axis=1)
            return vx_cache[hk, par]

        def scores(pair):
            hk = (2 * pair) // Q_PER_KV
            q2 = q[:, pair * LANES:(pair + 1) * LANES] * (HEAD_DIM ** -0.5)
            return [lax.dot_general(_keep_half(q2, par == 0),
                                    k_pair[hk // 2] if hk % 2 == par else k_swap[hk // 2], _NT,
                                    preferred_element_type=F32) for par in range(2)]

        n_pairs_q = N_Q_HEADS // 2
        s_next = scores(0)
        for pair in range(n_pairs_q):
            hk = (2 * pair) // Q_PER_KV
            s_pair = s_next
            if pair + 1 < n_pairs_q:
                s_next = scores(pair + 1)
            acc = None
            sink_terms = []
            for par in range(2):
                qh = 2 * pair + par
                slope = 2.0 ** (-8.0 * (qh + 1) / N_Q_HEADS)
                s2 = s_pair[par]
                s = jnp.where(from_prev, s2[:, :blk], s2[:, blk:]) - slope * dist
                if n == 0:
                    s = jnp.where(hide, -jnp.inf, s)
                sink = sink_ref[0, qh]
                m = jnp.maximum(jnp.max(s, axis=-1, keepdims=True), sink)
                p = jnp.exp(s - m)
                p2 = jnp.concatenate([jnp.where(from_prev, p, 0.0), jnp.where(from_prev, 0.0, p)], axis=1)
                r = jnp.dot(p2.astype(BF16), value_operand(hk, par), preferred_element_type=F32)
                acc = r if acc is None else acc + r
                sink_terms.append(jnp.exp(sink - m))
            den = acc[:, LANES:] + jnp.where(low, sink_terms[0], sink_terms[1])
            att_scr[pl.ds(r0, blk), pair * LANES:(pair + 1) * LANES] = (acc[:, :LANES] / den).astype(BF16)
        return carry

    for n in range(TOKEN_TILE // blk):
        block_body(n, 0)
    o_ref[...] = h_ref[...] + jnp.dot(att_scr[...], wo_ref[...], preferred_element_type=F32)


def _attention_layer(h, gain, w_in, w_out, sinks, seq_len):
    n = h.shape[0]
    tm = TOKEN_TILE
    n_tiles = n // tm
    qkv = pl.pallas_call(
        _qkv_kernel,
        out_shape=jax.ShapeDtypeStruct((n, Q_COLS + KV_COLS), BF16),
        grid=(n_tiles,),
        in_specs=[pl.BlockSpec((tm, D_MODEL), lambda i: (i, 0)),
                  _resident((1, D_MODEL)),
                  _resident((D_MODEL, Q_COLS + KV_COLS))],
        out_specs=pl.BlockSpec((tm, Q_COLS + KV_COLS), lambda i: (i, 0)),
        compiler_params=_params(),
        name="attn_qkv",
    )(h, gain.reshape(1, D_MODEL), w_in.astype(BF16))

    blocks_per_tile = tm // ATTN_BLOCK
    kv_col_block = Q_COLS // KV_COLS
    return pl.pallas_call(
        functools.partial(_attn_kernel, tiles_per_seq=seq_len // tm),
        out_shape=jax.ShapeDtypeStruct((n, D_MODEL), F32),
        grid=(n_tiles,),
        in_specs=[pl.BlockSpec((tm, Q_COLS), lambda i: (i, 0)),
                  pl.BlockSpec((tm, KV_COLS), lambda i: (i, kv_col_block)),
                  pl.BlockSpec((ATTN_BLOCK, KV_COLS),
                               lambda i: (jnp.maximum(i * blocks_per_tile - 1, 0), kv_col_block)),
                  pl.BlockSpec((tm, D_MODEL), lambda i: (i, 0)),
                  _resident((Q_COLS, D_MODEL)),
                  pl.BlockSpec(memory_space=pltpu.SMEM)],
        out_specs=pl.BlockSpec((tm, D_MODEL), lambda i: (i, 0)),
        scratch_shapes=[pltpu.VMEM((tm + ATTN_BLOCK, KV_COLS), BF16),
                        pltpu.VMEM((tm, Q_COLS), BF16)],
        compiler_params=_params(),
        name="attn_core",
    )(qkv, qkv, qkv, h, w_out.astype(BF16), sinks.reshape(1, N_Q_HEADS).astype(F32))


def _row_of_group(b, group, j):
    c = b.shape[0]
    b3 = b.reshape(c // group, group, LANES)
    return jnp.broadcast_to(b3[:, j:j + 1, :], b3.shape).reshape(c, LANES)


def _side_select(q, k, half, row):
    c = q.shape[0]
    if half >= SUBLANES:
        parts = [(q if (r // half) % 2 else k)[r:r + half] for r in range(0, c, half)]
        return jnp.concatenate(parts, axis=0)
    return jnp.where(row % (2 * half) >= half, q, k)


def _prefix_rows(g, row):
    b = g
    for sh in (1, 2, 4, 8, 16, 32):
        b = b + jnp.where(row >= sh, pltpu.roll(b, sh, 0), 0.0)
    return b


def _hgrn_chunk(q, k, g, v, st, row, level_of):
    c = HG_CHUNK
    b = _prefix_rows(g, row)
    sub = row % SUBLANES

    def mid_row(half):
        if half == 1:
            return jnp.where(sub % 2 == 1, pltpu.roll(b, 1, 0), b)
        if half == 2:
            return jnp.where(sub < 4, _row_of_group(b, 8, 1), _row_of_group(b, 8, 5))
        return _row_of_group(b, 2 * half, half - 1)

    a = lax.dot_general(q.astype(BF16), k.astype(BF16), _NT, preferred_element_type=F32)
    a = jnp.where(level_of == 0, a, 0.0)
    for lvl in range(1, 7):
        half = 1 << (lvl - 1)
        z = (_side_select(q, k, half, row) * jnp.exp(-jnp.abs(b - mid_row(half)))).astype(BF16)
        a = jnp.where(level_of == lvl, lax.dot_general(z, z, _NT, preferred_element_type=F32), a)
    o = jnp.dot(a.astype(BF16), v, preferred_element_type=F32)
    o = o + lax.dot_general((q * jnp.exp(b)).astype(BF16), st.astype(BF16), _NT, preferred_element_type=F32)
    b_last = b[c - 1:c, :]
    k_dec = (k * jnp.exp(b_last - b)).astype(BF16)
    st_new = st * jnp.exp(b_last) + lax.dot_general(v, k_dec, _TN, preferred_element_type=F32)
    return o, st_new


def _hgrn_kernel(h_ref, gain_ref, wq_ref, wf_ref, wi_ref, wg_ref, lbl_ref, ng_ref, wo_ref, o_ref,
                 xn_scr, q_scr, k_scr, g_scr, v_scr, gate_scr, y_scr, yb_scr, st_scr,
                 *, layer, tiles_per_seq):
    i = pl.program_id(0)
    c = HG_CHUNK
    n_chunks = TOKEN_TILE // c
    x = h_ref[...]
    xn_scr[...] = _rmsnorm(x, gain_ref[...]).astype(BF16)

    logits = lbl_ref[...]
    ex = jnp.exp(logits - jnp.max(logits, axis=0, keepdims=True))
    sm = ex / jnp.sum(ex, axis=0, keepdims=True)
    cum = sm[0:1, :]
    for r in range(1, layer + 1):
        cum = cum + sm[r:r + 1, :]
    lb = cum - sm[0:1, :]
    log_lb = jnp.log(lb)
    log_1m = jnp.log1p(-lb)

    @pl.when((i % tiles_per_seq) == 0)
    def _():
        st_scr[...] = jnp.zeros_like(st_scr)

    row = lax.broadcasted_iota(jnp.int32, (c, LANES), 0)
    ti = lax.broadcasted_iota(jnp.int32, (c, c), 0)
    si = lax.broadcasted_iota(jnp.int32, (c, c), 1)
    diff = ti ^ si
    level_of = sum((diff >= (1 << bit)).astype(jnp.int32) for bit in range(6))
    level_of = jnp.where(si > ti, -1, level_of)
    fine_bits = HG_FINE.bit_length() - 1
    in_fine_block = (level_of >= 0) & (level_of <= fine_bits)
    across_blocks = level_of == fine_bits + 1

    def project(cols):
        xn = xn_scr[...]
        g_scr[:, cols] = jnp.dot(xn, wf_ref[:, cols], preferred_element_type=F32)
        q_scr[:, cols] = jnp.dot(xn, wq_ref[:, cols], preferred_element_type=F32)
        v_scr[:, cols] = jnp.dot(xn, wi_ref[:, cols], preferred_element_type=F32).astype(BF16)
        gate_scr[:, cols] = jnp.dot(xn, wg_ref[:, cols], preferred_element_type=F32)

    blocked_ok = (HG_FINE - 1) * jnp.max(-log_lb) <= HG_MAX_EXPONENT

    @pl.when(blocked_ok)
    def _():
        def tile_work(ci, hh):
            rows = slice(ci * c, (ci + 1) * c)
            cs = slice(hh * HG_DIM, (hh + 1) * HG_DIM)
            lbh = lb[:, cs]
            f = g_scr[rows, cs]
            q = _silu(q_scr[rows, cs])
            t = jnp.exp(-jnp.abs(f))
            r = 1.0 / (1.0 + t)
            tr = t * r
            g = jnp.log(lbh + (1.0 - lbh) * jnp.where(f >= 0.0, r, tr))
            k = (1.0 - lbh) * jnp.where(f >= 0.0, tr, r)
            bsum = _prefix_rows(g, row)
            d = bsum - _row_of_group(bsum, HG_FINE, 0)
            a = lax.dot_general((q * jnp.exp(d)).astype(BF16), (k * jnp.exp(-d)).astype(BF16), _NT,
                                preferred_element_type=F32)
            mid = _row_of_group(bsum, 2 * HG_FINE, HG_FINE - 1)
            z = (_side_select(q, k, HG_FINE, row) * jnp.exp(-jnp.abs(bsum - mid))).astype(BF16)
            a_z = lax.dot_general(z, z, _NT, preferred_element_type=F32)
            v = v_scr[rows, cs]
            b_last = bsum[c - 1:c, :]
            k_dec = (k * jnp.exp(b_last - bsum)).astype(BF16)
            u = lax.dot_general(v, k_dec, _TN, preferred_element_type=F32)
            return dict(a=a, a_z=a_z, v=v, u=u, qe=(q * jnp.exp(bsum)).astype(BF16), ebl=jnp.exp(b_last))

        def tile_output(ci, hh, t, inter):
            rows = slice(ci * c, (ci + 1) * c)
            cs = slice(hh * HG_DIM, (hh + 1) * HG_DIM)
            a = jnp.where(across_blocks, t["a_z"], jnp.where(in_fine_block, t["a"], 0.0))
            y_scr[rows, cs] = jnp.dot(a.astype(BF16), t["v"], preferred_element_type=F32) + inter

        pair_cols = 2 * HG_DIM
        n_pairs = HG_HEADS // 2
        project(slice(0, pair_cols))
        for p in range(n_pairs):
            if p + 1 < n_pairs:
                project(slice((p + 1) * pair_cols, (p + 2) * pair_cols))
            heads = (2 * p, 2 * p + 1)
            tiles = {(ci, hh): tile_work(ci, hh) for ci in range(n_chunks) for hh in heads}
            inter = {}
            for hh in heads:
                st = st_scr[hh]
                for ci in range(n_chunks):
                    t = tiles[ci, hh]
                    inter[ci, hh] = lax.dot_general(t["qe"], st.astype(BF16), _NT, preferred_element_type=F32)
                    st = st * t["ebl"] + t["u"]
                st_scr[hh] = st
            for key, t in tiles.items():
                tile_output(*key, t, inter[key])

    @pl.when(jnp.logical_not(blocked_ok))
    def _():
        project(slice(0, D_MODEL))
        q_scr[...] = _silu(q_scr[...])
        f = g_scr[...]
        log_sig = jnp.minimum(f, 0.0) - jnp.log1p(jnp.exp(-jnp.abs(f)))
        u = log_1m + log_sig
        g_scr[...] = jnp.maximum(log_lb, u) + jnp.log1p(jnp.exp(-jnp.abs(log_lb - u)))
        k_scr[...] = (1.0 - lb) * jax.nn.sigmoid(-f)

        def chunk_body(ci, carry):
            r0 = pl.multiple_of(ci * c, c)
            for hh in range(HG_HEADS):
                cs = slice(hh * HG_DIM, (hh + 1) * HG_DIM)
                o, st_new = _hgrn_chunk(q_scr[pl.ds(r0, c), cs], k_scr[pl.ds(r0, c), cs],
                                        g_scr[pl.ds(r0, c), cs], v_scr[pl.ds(r0, c), cs], st_scr[hh],
                                        row, level_of)
                st_scr[hh] = st_new
                y_scr[pl.ds(r0, c), cs] = o
            return carry
        lax.fori_loop(0, n_chunks, chunk_body, 0)

    for hh in range(HG_HEADS):
        cs = slice(hh * HG_DIM, (hh + 1) * HG_DIM)
        yb_scr[:, cs] = (_rmsnorm(y_scr[:, cs], ng_ref[...]) * _silu(gate_scr[:, cs])).astype(BF16)
    o_ref[...] = x + jnp.dot(yb_scr[...], wo_ref[...], preferred_element_type=F32)


def _hgrn_layer(h, gain, w_in, w_out, norm_g, lb_logits, layer, seq_len):
    assert 2 * HG_FINE == HG_CHUNK
    n = h.shape[0]
    tm = TOKEN_TILE
    d = D_MODEL
    w = w_in.astype(BF16)
    tile = pl.BlockSpec((tm, d), lambda i: (i, 0))
    f32_tile = pltpu.VMEM((tm, d), F32)
    bf16_tile = pltpu.VMEM((tm, d), BF16)
    return pl.pallas_call(
        functools.partial(_hgrn_kernel, layer=layer, tiles_per_seq=seq_len // tm),
        out_shape=jax.ShapeDtypeStruct((n, d), F32),
        grid=(n // tm,),
        in_specs=[tile, _resident((1, d)),
                  _resident((d, d)), _resident((d, d)), _resident((d, d)), _resident((d, d)),
                  _resident((DEPTH, d)), _resident((1, HG_DIM)), _resident((d, d))],
        out_specs=tile,
        scratch_shapes=[bf16_tile, f32_tile, f32_tile, f32_tile, bf16_tile, f32_tile, f32_tile, bf16_tile,
                        pltpu.VMEM((HG_HEADS, HG_DIM, HG_DIM), F32)],
        compiler_params=_params(),
        name=f"hgrn_l{layer}",
    )(h, gain.reshape(1, d), w[:, 0:d], w[:, d:2 * d], w[:, 2 * d:3 * d], w[:, 3 * d:4 * d],
      lb_logits.astype(F32), norm_g.reshape(1, HG_DIM), w_out.astype(BF16))


def _ffn_kernel(h_ref, gain_ref, wu_ref, cw_ref, cb_ref, wd_ref, gfin_ref, o_ref,
                xn_scr, acc_scr, u_scr, halo_scr, *, tiles_per_seq, final_norm):
    i = pl.program_id(0)
    tm = TOKEN_TILE
    hr = SUBLANES
    n_lane_tiles = FF_CHUNK // LANES
    x = h_ref[...]
    xn_scr[...] = _rmsnorm(x, gain_ref[...]).astype(BF16)
    seq_first = (i % tiles_per_seq) == 0

    def up_proj(j):
        xn = xn_scr[...]
        for part in range(2):
            col0 = part * D_FF + j * FF_CHUNK
            u = jnp.dot(xn, wu_ref[:, col0:col0 + FF_CHUNK], preferred_element_type=F32)
            for c in range(n_lane_tiles):
                u_scr[j % 2, part, c, hr:, :] = u[:, c * LANES:(c + 1) * LANES]

    def conv(j, part):
        cols = []
        for c in range(n_lane_tiles):
            us = u_scr.at[j % 2, part, c]
            us[0:hr, :] = jnp.where(seq_first, 0.0, halo_scr[j, part, c])
            halo_scr[j, part, c] = us[tm:tm + hr, :]
            col0 = part * D_FF + j * FF_CHUNK + c * LANES
            lanes = slice(col0, col0 + LANES)
            cols.append(cb_ref[:, lanes] + cw_ref[0:1, lanes] * us[hr - 2:hr - 2 + tm, :]
                        + cw_ref[1:2, lanes] * us[hr - 1:hr - 1 + tm, :] + cw_ref[2:3, lanes] * us[hr:hr + tm, :])
        return jnp.concatenate(cols, axis=-1)

    def down_proj(j, act):
        dn = jnp.dot(act, wd_ref[j * FF_CHUNK:(j + 1) * FF_CHUNK, :], preferred_element_type=F32)
        if j == 0:
            acc_scr[...] = dn
        else:
            acc_scr[...] += dn

    up_proj(0)
    act_prev = None
    for j in range(N_FF_CHUNKS):
        if j + 1 < N_FF_CHUNKS:
            up_proj(j + 1)
        if act_prev is not None:
            down_proj(j - 1, act_prev)
        act_prev = (_silu(conv(j, 0)) * conv(j, 1)).astype(BF16)
    down_proj(N_FF_CHUNKS - 1, act_prev)
    y = x + acc_scr[...]
    if final_norm:
        y = _rmsnorm(y, gfin_ref[...])
    o_ref[...] = y


def _ffn_layer(h, gain, w_up, conv_w, conv_b, w_down, gain_final, final_norm, seq_len):
    n = h.shape[0]
    tm = TOKEN_TILE
    d = D_MODEL
    tile = pl.BlockSpec((tm, d), lambda i: (i, 0))
    nc, fc = N_FF_CHUNKS, FF_CHUNK
    return pl.pallas_call(
        functools.partial(_ffn_kernel, tiles_per_seq=seq_len // tm, final_norm=final_norm),
        out_shape=jax.ShapeDtypeStruct((n, d), F32),
        grid=(n // tm,),
        in_specs=[tile, _resident((1, d)), _resident((d, 2 * D_FF)), _resident((3, 2 * D_FF)),
                  _resident((1, 2 * D_FF)), _resident((D_FF, d)), _resident((1, d))],
        out_specs=tile,
        scratch_shapes=[pltpu.VMEM((tm, d), BF16), pltpu.VMEM((tm, d), F32),
                        pltpu.VMEM((2, 2, fc // LANES, tm + SUBLANES, LANES), F32),
                        pltpu.VMEM((nc, 2, fc // LANES, SUBLANES, LANES), F32)],
        compiler_params=_params(),
        name="conv_ffn",
    )(h, gain.reshape(1, d), w_up.astype(BF16), conv_w, conv_b.reshape(1, 2 * D_FF), w_down.astype(BF16),
      gain_final.reshape(1, d))


def kernel(x, norm_mix, norm_ffn, norm_final, attn_w_in, attn_w_out, attn_sinks, hgrn_w_in, hgrn_w_out,
           hgrn_norm, hgrn_lb_logits, ffn_w_up, ffn_conv_w, ffn_conv_b, ffn_w_down):
    bsz, seq_len, d = x.shape
    assert d == D_MODEL and seq_len % TOKEN_TILE == 0
    h = x.reshape(bsz * seq_len, d)
    for layer in range(DEPTH):
        idx = layer // 2
        if layer % 2 == 0:
            h = _attention_layer(h, norm_mix[layer], attn_w_in[idx], attn_w_out[idx], attn_sinks[idx], seq_len)
        else:
            h = _hgrn_layer(h, norm_mix[layer], hgrn_w_in[idx], hgrn_w_out[idx], hgrn_norm[idx],
                            hgrn_lb_logits, layer, seq_len)
        h = _ffn_layer(h, norm_ffn[layer], ffn_w_up[layer], ffn_conv_w[layer], ffn_conv_b[layer],
                       ffn_w_down[layer], norm_final, layer == DEPTH - 1, seq_len)
    return h.reshape(bsz, seq_len, d)
```
